```python
import jax, jax.numpy as jnp
from jax import lax
import numpy as np

D_MODEL = 1024
BATCH = 8
SEQ = 4096
DEPTH = 1
DEC_BATCH = 128
DEC_SEQ = 8
PAST_LEN = 16384
PAGE_SIZE = 128

N_HEADS = 8
D_NOPE = 64
D_ROPE = 32
D_V = 64
D_C = 256
D_CQ = 384
D_ATT = N_HEADS * D_V
D_CONV = 512
CONV_W = 31
D_MIX = D_ATT + D_CONV
D_IN = D_CQ + D_C + D_ROPE + 2 * D_CONV
D_FF = 2816
FFN_CONV_W = 3
ROPE_THETA = 10000.0
Q_BLOCK = 128
EPS = 1e-6
ATTN_SCALE = (D_NOPE + D_ROPE) ** -0.5

kernel_name = 'hymba_mla_conformer_convffn_step'


def rmsnorm(x, g):
    xf = x.astype(jnp.float32)
    y = xf * lax.rsqrt(jnp.mean(xf * xf, axis=-1, keepdims=True) + EPS)
    return (y * g.astype(jnp.float32)).astype(x.dtype)


def layernorm(x, g, b):
    xf = x.astype(jnp.float32)
    mu = jnp.mean(xf, axis=-1, keepdims=True)
    xc = xf - mu
    y = xc * lax.rsqrt(jnp.mean(xc * xc, axis=-1, keepdims=True) + EPS)
    return (y * g.astype(jnp.float32) + b.astype(jnp.float32)).astype(x.dtype)


def rope(x, pos):
    half = D_ROPE // 2
    inv = ROPE_THETA ** (-(jnp.arange(half, dtype=jnp.float32) * 2.0 / D_ROPE))
    ang = pos.astype(jnp.float32)[:, None] * inv[None, :]
    cos = jnp.cos(ang)[None, :, None, :]
    sin = jnp.sin(ang)[None, :, None, :]
    xf = x.astype(jnp.float32)
    x1, x2 = xf[..., :half], xf[..., half:]
    return jnp.concatenate([x1 * cos - x2 * sin, x2 * cos + x1 * sin], axis=-1).astype(x.dtype)


def causal_dwconv(x, prev, w, b):
    xp = jnp.concatenate([prev, x], axis=1)
    y = lax.conv_general_dilated(xp, w[:, None, :], window_strides=(1,), padding='VALID',
                                 dimension_numbers=('NWC', 'WIO', 'NWC'),
                                 feature_group_count=x.shape[-1])
    return y + b, xp[:, -(w.shape[0] - 1):]


def mixer_inputs(h, pos, w_in, g_q_norm, w_uq, g_kv_norm, w_uk):
    proj = h @ w_in
    o1, o2, o3 = D_CQ, D_CQ + D_C, D_CQ + D_C + D_ROPE
    cq, ckv, kr, glu_in = proj[..., :o1], proj[..., o1:o2], proj[..., o2:o3], proj[..., o3:]
    q = jnp.einsum('btc,chd->bthd', rmsnorm(cq, g_q_norm), w_uq)
    q_rope = rope(q[..., D_NOPE:], pos)
    q_lat = jnp.einsum('bthn,chn->bthc', q[..., :D_NOPE], w_uk)
    kr = rope(kr[:, :, None, :], pos)[:, :, 0]
    kv_rows = jnp.concatenate([rmsnorm(ckv, g_kv_norm), kr], axis=-1)
    glu = glu_in[..., :D_CONV] * jax.nn.sigmoid(glu_in[..., D_CONV:])
    return q_lat, q_rope, kv_rows, glu


def mla_attend(q_lat, q_rope, kv, q_pos, k_pos):
    ckv, kr = kv[..., :D_C], kv[..., D_C:]
    s = jnp.einsum('bthc,bsc->bhts', q_lat, ckv) + jnp.einsum('bthr,bsr->bhts', q_rope, kr)
    s = s.astype(jnp.float32) * ATTN_SCALE
    mask = k_pos[None, :] <= q_pos[:, None]
    s = jnp.where(mask[None, None], s, -jnp.inf)
    p = jax.nn.softmax(s, axis=-1).astype(ckv.dtype)
    return jnp.einsum('bhts,bsc->bthc', p, ckv)


def prompt_attention(q_lat, q_rope, kv_rows, pos):
    b, s = q_lat.shape[0], q_lat.shape[1]
    nb = s // Q_BLOCK
    ql = q_lat.reshape(b, nb, Q_BLOCK, N_HEADS, D_C).transpose(1, 0, 2, 3, 4)
    qr = q_rope.reshape(b, nb, Q_BLOCK, N_HEADS, D_ROPE).transpose(1, 0, 2, 3, 4)
    qp = pos.reshape(nb, Q_BLOCK)

    def block(args):
        ql_b, qr_b, qp_b = args
        return mla_attend(ql_b, qr_b, kv_rows, qp_b, pos)

    out = lax.map(block, (ql, qr, qp))
    return out.transpose(1, 0, 2, 3, 4).reshape(b, s, N_HEADS, D_C)


def conv_module(glu, prev, w_dw, b_dw, g_ln, b_ln):
    y, new_prev = causal_dwconv(glu, prev, w_dw, b_dw)
    return jax.nn.silu(layernorm(y, g_ln, b_ln)), new_prev


def merge_and_ffn(x, o_lat, conv_y, ffn_prev, w_uv, g_attn_out, g_conv_out, w_o,
                  g_ffn_norm, w_up, w_ffn_dw, b_ffn_dw, w_down):
    b, t = x.shape[0], x.shape[1]
    attn = jnp.einsum('bthc,chv->bthv', o_lat, w_uv).reshape(b, t, D_ATT)
    mixed = jnp.concatenate([rmsnorm(attn, g_attn_out), rmsnorm(conv_y, g_conv_out)], axis=-1)
    x = x + mixed @ w_o
    u, new_prev = causal_dwconv(rmsnorm(x, g_ffn_norm) @ w_up, ffn_prev, w_ffn_dw, b_ffn_dw)
    x = x + (jax.nn.silu(u[..., :D_FF]) * u[..., D_FF:]) @ w_down
    return x, new_prev


def setup_inputs(seed: int = 0) -> dict:
    key = jax.random.key(seed)
    ks = jax.random.split(key, 32)
    f32 = jnp.float32

    def nrm(k, shape, scale):
        return jax.random.normal(k, shape, f32) * scale

    def gain(k, shape):
        return 1.0 + 0.02 * jax.random.normal(k, shape, f32)

    n_pages = PAST_LEN // PAGE_SIZE
    n_used = DEC_BATCH * n_pages
    n_pool = n_used + max(n_used // 4, 1)
    page_table = jax.random.permutation(ks[0], n_pool)[:n_used].reshape(DEC_BATCH, n_pages).astype(jnp.int32)
    L = DEPTH
    return {
        'x_prompt': nrm(ks[1], (BATCH, SEQ, D_MODEL), 1.0),
        'x_sample': nrm(ks[2], (DEC_BATCH, DEC_SEQ, D_MODEL), 1.0),
        'cache_kv_latent': nrm(ks[3], (L, n_pool, PAGE_SIZE, D_C + D_ROPE), 1.0),
        'state_conv': nrm(ks[4], (L, DEC_BATCH, CONV_W - 1, D_CONV), 0.5),
        'state_ffn_conv': nrm(ks[5], (L, DEC_BATCH, FFN_CONV_W - 1, 2 * D_FF), 1.0),
        'page_table': page_table,
        'g_attn_norm': gain(ks[6], (L, D_MODEL)),
        'w_in': nrm(ks[7], (L, D_MODEL, D_IN), D_MODEL ** -0.5),
        'g_q_norm': gain(ks[8], (L, D_CQ)),
        'w_uq': nrm(ks[9], (L, D_CQ, N_HEADS, D_NOPE + D_ROPE), D_CQ ** -0.5),
        'g_kv_norm': gain(ks[10], (L, D_C)),
        'w_uk': nrm(ks[11], (L, D_C, N_HEADS, D_NOPE), D_C ** -0.5),
        'w_uv': nrm(ks[12], (L, D_C, N_HEADS, D_V), D_C ** -0.5),
        'w_dw': nrm(ks[13], (L, CONV_W, D_CONV), CONV_W ** -0.5),
        'b_dw': nrm(ks[14], (L, D_CONV), 0.02),
        'g_conv_ln': gain(ks[15], (L, D_CONV)),
        'b_conv_ln': nrm(ks[16], (L, D_CONV), 0.02),
        'g_attn_out': gain(ks[17], (L, D_ATT)),
        'g_conv_out': gain(ks[18], (L, D_CONV)),
        'w_o': nrm(ks[19], (L, D_MIX, D_MODEL), D_MIX ** -0.5),
        'g_ffn_norm': gain(ks[20], (L, D_MODEL)),
        'w_up': nrm(ks[21], (L, D_MODEL, 2 * D_FF), D_MODEL ** -0.5),
        'w_ffn_dw': nrm(ks[22], (L, FFN_CONV_W, 2 * D_FF), FFN_CONV_W ** -0.5),
        'b_ffn_dw': nrm(ks[23], (L, 2 * D_FF), 0.02),
        'w_down': nrm(ks[24], (L, D_FF, D_MODEL), D_FF ** -0.5),
        'g_final': gain(ks[25], (D_MODEL,)),
    }


def reference(x_prompt, x_sample, cache_kv_latent, state_conv, state_ffn_conv, page_table,
              g_attn_norm, w_in, g_q_norm, w_uq, g_kv_norm, w_uk, w_uv, w_dw, b_dw,
              g_conv_ln, b_conv_ln, g_attn_out, g_conv_out, w_o, g_ffn_norm, w_up,
              w_ffn_dw, b_ffn_dw, w_down, g_final):
    n_pages = page_table.shape[1]
    pos_p = jnp.arange(SEQ, dtype=jnp.int32)
    pos_s = PAST_LEN + jnp.arange(DEC_SEQ, dtype=jnp.int32)
    k_pos_s = jnp.concatenate([jnp.arange(n_pages * PAGE_SIZE, dtype=jnp.int32), pos_s])
    y_p, y_s = x_prompt, x_sample
    kv_p_new, conv_p_new, ffn_p_new = [], [], []
    kv_s_new, conv_s_new, ffn_s_new = [], [], []
    for l in range(DEPTH):
        mix_w = (w_in[l], g_q_norm[l], w_uq[l], g_kv_norm[l], w_uk[l])
        conv_w = (w_dw[l], b_dw[l], g_conv_ln[l], b_conv_ln[l])
        out_w = (w_uv[l], g_attn_out[l], g_conv_out[l], w_o[l], g_ffn_norm[l], w_up[l],
                 w_ffn_dw[l], b_ffn_dw[l], w_down[l])
        q_lat, q_rope, kv_rows, glu = mixer_inputs(rmsnorm(y_p, g_attn_norm[l]), pos_p, *mix_w)
        o_lat = prompt_attention(q_lat, q_rope, kv_rows, pos_p)
        conv_y, conv_st = conv_module(glu, jnp.zeros((y_p.shape[0], CONV_W - 1, D_CONV), glu.dtype), *conv_w)
        y_p, ffn_st = merge_and_ffn(y_p, o_lat, conv_y,
                                    jnp.zeros((y_p.shape[0], FFN_CONV_W - 1, 2 * D_FF), y_p.dtype), *out_w)
        kv_p_new.append(kv_rows)
        conv_p_new.append(conv_st)
        ffn_p_new.append(ffn_st)
        q_lat, q_rope, kv_rows, glu = mixer_inputs(rmsnorm(y_s, g_attn_norm[l]), pos_s, *mix_w)
        past = jnp.take(cache_kv_latent[l], page_table, axis=0).reshape(
            page_table.shape[0], n_pages * PAGE_SIZE, D_C + D_ROPE)
        keys = jnp.concatenate([past, kv_rows], axis=1)
        o_lat = mla_attend(q_lat, q_rope, keys, pos_s, k_pos_s)
        conv_y, conv_st = conv_module(glu, state_conv[l], *conv_w)
        y_s, ffn_st = merge_and_ffn(y_s, o_lat, conv_y, state_ffn_conv[l], *out_w)
        kv_s_new.append(kv_rows)
        conv_s_new.append(conv_st)
        ffn_s_new.append(ffn_st)
    y_prompt = rmsnorm(y_p, g_final)
    y_sample = rmsnorm(y_s, g_final)
    new_kv_prompt = jnp.stack(kv_p_new)
    new_conv_prompt = jnp.stack(conv_p_new)
    new_ffn_prompt = jnp.stack(ffn_p_new)
    new_kv_sample = jnp.stack(kv_s_new)
    new_conv_sample = jnp.stack(conv_s_new)
    new_ffn_sample = jnp.stack(ffn_s_new)
    return (y_prompt, y_sample, new_kv_prompt, new_conv_prompt, new_ffn_prompt,
            new_kv_sample, new_conv_sample, new_ffn_sample)
```

```python
import functools

import jax
import jax.numpy as jnp
from jax import lax
from jax.experimental import pallas as pl
from jax.experimental.pallas import tpu as pltpu

N_HEADS = 8
D_NOPE = 64
D_ROPE = 32
D_V = 64
D_C = 256
D_CQ = 384
D_ATT = N_HEADS * D_V
D_CONV = 512
CONV_W = 31
D_FF = 2816
FFN_CONV_W = 3
ROPE_THETA = 10000.0
PAGE_SIZE = 128
EPS = 1e-6
ATTN_SCALE = (D_NOPE + D_ROPE) ** -0.5

LANE = 128
SUBLANES = 8
D_QK = D_C + D_ROPE
D_QK_PAD = D_C + LANE
Q_TILE = 128
HEAD_GROUP = 256 // D_NOPE
FFN_CHUNKS = 2
FFN_CHUNK = D_FF // FFN_CHUNKS
PAGES_PER_STEP = 16
SAMPLE_FFN_SEQS = 32
VMEM_LIMIT = 56 * 1024 * 1024

_BF16 = jnp.bfloat16
_F32 = jnp.float32


def _rms(x, g):
    return x * lax.rsqrt(jnp.mean(x * x, axis=-1, keepdims=True) + EPS) * g


def _dot(a, b):
    return jnp.dot(a, b, preferred_element_type=_F32)


def _dot_nt(a, b):
    return lax.dot_general(a, b, (((1,), (1,)), ((), ())), preferred_element_type=_F32)


def _const_spec(shape):
    nd = len(shape)
    return pl.BlockSpec(shape, lambda *_: (0,) * nd, pipeline_mode=pl.Buffered(1))


def _inproj_kernel(x_ref, cos_ref, sin_ref, g_in_ref, w_in_ref, g_q_ref, w_qn_ref, w_qr_ref,
                   w_qs_ref, g_kv_ref, w_uk_ref, q_ref, kv_ref, kvb_ref, glu_ref, *, tq):
    tm = x_ref.shape[0]
    cos = cos_ref[...]
    sin = sin_ref[...]
    h = _rms(x_ref[...], g_in_ref[...]).astype(_BF16)
    proj = _dot(h, w_in_ref[...])
    o_kv, o_kr, o_ks, o_ga, o_gb = D_CQ, D_CQ + D_C, D_CQ + D_C + LANE, D_CQ + D_C + 2 * LANE, D_CQ + D_C + 2 * LANE + D_CONV
    cq = proj[:, :o_kv]
    ckv = proj[:, o_kv:o_kr]
    glu_ref[...] = proj[:, o_ga:o_gb] * jax.nn.sigmoid(proj[:, o_gb:])
    ckv_n = _rms(ckv, g_kv_ref[...])
    kr = proj[:, o_kr:o_ks] * cos + proj[:, o_ks:o_ga] * sin
    kv_ref[:, :D_C] = ckv_n
    kv_ref[:, D_C:] = kr[:, :D_ROPE]
    kvb_ref[:, :D_C] = ckv_n.astype(_BF16)
    kvb_ref[:, D_C:] = kr.astype(_BF16)
    cqn = _rms(cq, g_q_ref[...]).astype(_BF16)
    qn = _dot(cqn, w_qn_ref[...]).astype(_BF16)
    qr = _dot(cqn, w_qr_ref[...])
    qs = _dot(cqn, w_qs_ref[...])
    for hd in range(N_HEADS):
        grp = hd // HEAD_GROUP
        ql = _dot(qn[:, grp * 256:(grp + 1) * 256], w_uk_ref[hd]).astype(_BF16)
        qrh = (qr[:, hd * LANE:(hd + 1) * LANE] * cos + qs[:, hd * LANE:(hd + 1) * LANE] * sin).astype(_BF16)
        for s in range(tm // tq):
            q_ref[s, hd, :, :D_C] = ql[s * tq:(s + 1) * tq]
            q_ref[s, hd, :, D_C:] = qrh[s * tq:(s + 1) * tq]


def _inproj(x, cos_tab, sin_tab, wts, *, tm, tq):
    rows = x.shape[0]
    n_tab = cos_tab.shape[0] // tm
    g_in, w_in, g_q, w_qn, w_qr, w_qs, g_kv, w_uk = wts
    row_spec = lambda w: pl.BlockSpec((tm, w), lambda i: (i, 0))
    tab_spec = pl.BlockSpec((tm, LANE), lambda i: (i % n_tab, 0))
    return pl.pallas_call(
        functools.partial(_inproj_kernel, tq=tq),
        grid=(rows // tm,),
        in_specs=[row_spec(x.shape[1]), tab_spec, tab_spec] + [_const_spec(w.shape) for w in wts],
        out_specs=[pl.BlockSpec((tm // tq, N_HEADS, tq, D_QK_PAD), lambda i: (i, 0, 0, 0)),
                   row_spec(D_QK), row_spec(D_QK_PAD), row_spec(D_CONV)],
        out_shape=[jax.ShapeDtypeStruct((rows // tq, N_HEADS, tq, D_QK_PAD), _BF16),
                   jax.ShapeDtypeStruct((rows, D_QK), _F32),
                   jax.ShapeDtypeStruct((rows, D_QK_PAD), _BF16),
                   jax.ShapeDtypeStruct((rows, D_CONV), _F32)],
        compiler_params=pltpu.CompilerParams(dimension_semantics=("parallel",), vmem_limit_bytes=VMEM_LIMIT),
        name="inproj",
    )(x, cos_tab, sin_tab, *wts)


CONV_HALO = 32
CONV_ROWS = 16


def _conv_post(y, g_ln, b_ln, g_out):
    mu = jnp.mean(y, axis=-1, keepdims=True)
    yc = y - mu
    z = yc * lax.rsqrt(jnp.mean(yc * yc, axis=-1, keepdims=True) + EPS) * g_ln + b_ln
    return _rms(z * jax.nn.sigmoid(z), g_out).astype(_BF16)


def _conv_prompt_kernel(cur_ref, halo_ref, w_ref, b_ref, g_ln_ref, b_ln_ref, g_out_ref, o_ref, ext_ref, sh_ref):
    tc = cur_ref.shape[1]
    first = pl.program_id(1) == 0
    ext_ref[:CONV_HALO] = jnp.where(first, 0.0, halo_ref[0])
    ext_ref[CONV_HALO:CONV_HALO + tc] = cur_ref[0]
    ext_ref[CONV_HALO + tc:] = jnp.zeros((SUBLANES, D_CONV), _F32)
    off = CONV_HALO - (CONV_W - 1)
    for r in range(SUBLANES):
        sh_ref[r] = ext_ref[r:r + sh_ref.shape[1], :]

    def body(it, carry):
        base = pl.multiple_of(it * CONV_ROWS, CONV_ROWS)
        acc = jnp.broadcast_to(b_ref[...], (CONV_ROWS, D_CONV))
        for k in range(CONV_W):
            r = (off + k) % SUBLANES
            acc = acc + w_ref[k:k + 1, :] * sh_ref[r, pl.ds(base + (off + k - r), CONV_ROWS), :]
        o_ref[pl.ds(base, CONV_ROWS), :] = _conv_post(acc, g_ln_ref[...], b_ln_ref[...], g_out_ref[...])
        return carry

    lax.fori_loop(0, tc // CONV_ROWS, body, 0)


def _conv_prompt(glu, wts, *, batch, seq, tc):
    glu3 = glu.reshape(batch, seq, D_CONV)
    hb = tc // CONV_HALO
    return pl.pallas_call(
        _conv_prompt_kernel,
        grid=(batch, seq // tc),
        in_specs=[pl.BlockSpec((1, tc, D_CONV), lambda b, i: (b, i, 0)),
                  pl.BlockSpec((1, CONV_HALO, D_CONV), lambda b, i: (b, jnp.maximum(i * hb - 1, 0), 0))]
                 + [_const_spec(w.shape) for w in wts],
        out_specs=pl.BlockSpec((tc, D_CONV), lambda b, i: (b * (seq // tc) + i, 0)),
        out_shape=jax.ShapeDtypeStruct((batch * seq, D_CONV), _BF16),
        scratch_shapes=[pltpu.VMEM((CONV_HALO + tc + SUBLANES, D_CONV), _F32),
                        pltpu.VMEM((SUBLANES, CONV_HALO + tc, D_CONV), _F32)],
        compiler_params=pltpu.CompilerParams(dimension_semantics=("parallel", "parallel"), vmem_limit_bytes=VMEM_LIMIT),
        name="conv_prompt",
    )(glu3, glu3, *wts)


def _conv_sample_kernel(ext_ref, w_ref, b_ref, g_ln_ref, b_ln_ref, g_out_ref, o_ref):
    steps, nb = o_ref.shape[0], o_ref.shape[1]
    chunks = nb // CONV_ROWS

    def body(it, carry):
        t = it // chunks
        base = pl.multiple_of((it % chunks) * CONV_ROWS, CONV_ROWS)
        acc = jnp.broadcast_to(b_ref[...], (CONV_ROWS, D_CONV))
        for k in range(CONV_W):
            acc = acc + w_ref[k:k + 1, :] * ext_ref[t + k, pl.ds(base, CONV_ROWS), :]
        o_ref[t, pl.ds(base, CONV_ROWS), :] = _conv_post(acc, g_ln_ref[...], b_ln_ref[...], g_out_ref[...])
        return carry

    lax.fori_loop(0, steps * chunks, body, 0)


def _conv_sample(ext, wts, *, steps):
    nb = ext.shape[1]
    return pl.pallas_call(
        _conv_sample_kernel,
        out_shape=jax.ShapeDtypeStruct((steps, nb, D_CONV), _BF16),
        compiler_params=pltpu.CompilerParams(vmem_limit_bytes=VMEM_LIMIT),
        name="conv_sample",
    )(ext, *wts)


def _softmax_step(s, m_prev, l_prev, acc_prev, v):
    m_new = jnp.maximum(m_prev, jnp.max(s, axis=-1, keepdims=True))
    alpha = jnp.exp(m_prev - m_new)
    p = jnp.exp(s - m_new)
    l_new = alpha * l_prev + jnp.sum(p, axis=-1, keepdims=True)
    acc_new = alpha * acc_prev + _dot(p.astype(_BF16), v)
    return m_new, l_new, acc_new


def _attn_out(o, rows_per_head, w_uv_ref, g_ref):
    parts = []
    for pr in range(N_HEADS // 2):
        acc = None
        for hd in (2 * pr, 2 * pr + 1):
            oh = o[hd * rows_per_head:(hd + 1) * rows_per_head].astype(_BF16)
            term = _dot(oh, w_uv_ref[hd])
            acc = term if acc is None else acc + term
        parts.append(acc)
    attn = jnp.concatenate(parts, axis=-1)
    return _rms(attn, g_ref[...]).astype(_BF16)


def _attn_prompt_kernel(q_ref, kv_ref, w_uv_ref, g_ref, o_ref, m_ref, l_ref, acc_ref, *, tk):
    tq = q_ref.shape[2]
    rows = N_HEADS * tq
    qi = pl.program_id(1)
    q = q_ref[0].reshape(rows, D_QK_PAD)
    m_ref[...] = jnp.full(m_ref.shape, -jnp.inf, _F32)
    l_ref[...] = jnp.zeros(l_ref.shape, _F32)
    acc_ref[...] = jnp.zeros(acc_ref.shape, _F32)
    n_full = (qi * tq) // tk

    def chunk(c, masked):
        start = pl.multiple_of(c * tk, tk)
        kc = kv_ref[0, pl.ds(start, tk), :]
        s = _dot_nt(q, kc) * ATTN_SCALE
        if masked:
            q_pos = qi * tq + lax.broadcasted_iota(jnp.int32, (rows, tk), 0) % tq
            k_pos = start + lax.broadcasted_iota(jnp.int32, (rows, tk), 1)
            s = jnp.where(k_pos <= q_pos, s, -jnp.inf)
        m, l, acc = _softmax_step(s, m_ref[...], l_ref[...], acc_ref[...], kc[:, :D_C])
        m_ref[...] = m
        l_ref[...] = l
        acc_ref[...] = acc

    def body(c, carry):
        chunk(c, False)
        return carry

    lax.fori_loop(0, n_full, body, 0)
    chunk(n_full, True)
    o_ref[...] = _attn_out(acc_ref[...] / l_ref[...], tq, w_uv_ref, g_ref)


def _attn_prompt(q, kvb, w_uv, g, *, batch, seq, tk):
    tq = q.shape[2]
    nq = seq // tq
    rows = N_HEADS * tq
    return pl.pallas_call(
        functools.partial(_attn_prompt_kernel, tk=tk),
        grid=(batch, nq),
        in_specs=[pl.BlockSpec((1, N_HEADS, tq, D_QK_PAD), lambda b, i: (b * nq + i, 0, 0, 0)),
                  pl.BlockSpec((1, seq, D_QK_PAD), lambda b, i: (b, 0, 0)),
                  _const_spec(w_uv.shape), _const_spec(g.shape)],
        out_specs=pl.BlockSpec((tq, D_ATT), lambda b, i: (b * nq + i, 0)),
        out_shape=jax.ShapeDtypeStruct((batch * seq, D_ATT), _BF16),
        scratch_shapes=[pltpu.VMEM((rows, 1), _F32), pltpu.VMEM((rows, 1), _F32), pltpu.VMEM((rows, D_C), _F32)],
        compiler_params=pltpu.CompilerParams(dimension_semantics=("parallel", "parallel"), vmem_limit_bytes=VMEM_LIMIT),
        name="attn_prompt",
    )(q, kvb.reshape(batch, seq, D_QK_PAD), w_uv, g)


def _attn_sample_kernel(pt_ref, q_ref, kvn_ref, w_uv_ref, g_ref, *rest, steps):
    page_refs = rest[:PAGES_PER_STEP]
    o_ref, k_ref, m_ref, l_ref, acc_ref = rest[PAGES_PER_STEP:]
    c = pl.program_id(1)
    rows = q_ref.shape[1]

    @pl.when(c == 0)
    def _():
        k_ref[...] = jnp.zeros(k_ref.shape, _BF16)
        m_ref[...] = jnp.full(m_ref.shape, -jnp.inf, _F32)
        l_ref[...] = jnp.zeros(l_ref.shape, _F32)
        acc_ref[...] = jnp.zeros(acc_ref.shape, _F32)

    for p, page_ref in enumerate(page_refs):
        k_ref[p * PAGE_SIZE:(p + 1) * PAGE_SIZE, :D_QK] = page_ref[0].astype(_BF16)
    keys = k_ref[...]
    s = _dot_nt(q_ref[0], keys) * ATTN_SCALE
    m, l, acc = _softmax_step(s, m_ref[...], l_ref[...], acc_ref[...], keys[:, :D_C])
    m_ref[...] = m
    l_ref[...] = l
    acc_ref[...] = acc

    @pl.when(c == pl.num_programs(1) - 1)
    def _():
        k_ref[:PAGE_SIZE, :] = jnp.zeros((PAGE_SIZE, D_QK_PAD), _BF16)
        k_ref[:steps, :] = kvn_ref[0]
        kn = k_ref[:PAGE_SIZE, :]
        sn = _dot_nt(q_ref[0], kn) * ATTN_SCALE
        t_pos = lax.broadcasted_iota(jnp.int32, (rows, PAGE_SIZE), 0) % steps
        j_pos = lax.broadcasted_iota(jnp.int32, (rows, PAGE_SIZE), 1)
        sn = jnp.where(j_pos <= t_pos, sn, -jnp.inf)
        _, l2, acc2 = _softmax_step(sn, m_ref[...], l_ref[...], acc_ref[...], kn[:, :D_C])
        o_ref[0] = _attn_out(acc2 / l2, steps, w_uv_ref, g_ref)


def _attn_sample(page_table, q, kv_new, cache, w_uv, g):
    nb, rows, _ = q.shape
    steps = kv_new.shape[1]
    n_pages = page_table.shape[1]
    n_chunks = n_pages // PAGES_PER_STEP

    def page_spec(p):
        return pl.BlockSpec((1, PAGE_SIZE, D_QK),
                            lambda b, c, pt: (pt[b * n_pages + c * PAGES_PER_STEP + p], 0, 0))

    const = lambda shape: pl.BlockSpec(shape, lambda b, c, pt: (0,) * len(shape), pipeline_mode=pl.Buffered(1))
    return pl.pallas_call(
        functools.partial(_attn_sample_kernel, steps=steps),
        grid_spec=pltpu.PrefetchScalarGridSpec(
            num_scalar_prefetch=1,
            grid=(nb, n_chunks),
            in_specs=[pl.BlockSpec((1, rows, D_QK_PAD), lambda b, c, pt: (b, 0, 0)),
                      pl.BlockSpec((1, steps, D_QK_PAD), lambda b, c, pt: (b, 0, 0)),
                      const(w_uv.shape), const(g.shape)]
                     + [page_spec(p) for p in range(PAGES_PER_STEP)],
            out_specs=pl.BlockSpec((1, steps, D_ATT), lambda b, c, pt: (b, 0, 0)),
            scratch_shapes=[pltpu.VMEM((PAGES_PER_STEP * PAGE_SIZE, D_QK_PAD), _BF16),
                            pltpu.VMEM((rows, 1), _F32), pltpu.VMEM((rows, 1), _F32),
                            pltpu.VMEM((rows, D_C), _F32)]),
        out_shape=jax.ShapeDtypeStruct((nb, steps, D_ATT), _BF16),
        compiler_params=pltpu.CompilerParams(dimension_semantics=("parallel", "arbitrary"), vmem_limit_bytes=VMEM_LIMIT),
        name="attn_sample",
    )(page_table.reshape(-1), q, kv_new, w_uv, g, *([cache] * PAGES_PER_STEP))


def _out_ffn_kernel(x_ref, attn_ref, conv_ref, prev_g_ref, prev_v_ref, w_oa_ref, w_oc_ref, g_ffn_ref,
                    w_up_g_ref, w_up_v_ref, w_dw_g_ref, w_dw_v_ref, b_dw_g_ref, b_dw_v_ref, w_down_ref, g_fin_ref,
                    y_ref, tail_g_ref, tail_v_ref, ubuf_ref, carry_ref, acc_ref, h2_ref, *, shift, tiles_per_seq):
    gb, sb, d_model = x_ref.shape
    tm = gb * sb
    hist = prev_g_ref.shape[1]
    j = pl.program_id(1)
    n_chunks = pl.num_programs(1)
    first = pl.program_id(0) % tiles_per_seq == 0

    @pl.when(j == 0)
    def _():
        flat = lambda ref: ref[...].reshape(tm, ref.shape[2])
        acc_ref[...] = flat(x_ref) + _dot(flat(attn_ref), w_oa_ref[...]) + _dot(flat(conv_ref), w_oc_ref[...])
        h2_ref[...] = _rms(acc_ref[...], g_ffn_ref[...]).astype(_BF16)

    halves = []
    for part, (prev_ref, w_up_ref, w_dw_ref, b_dw_ref, tail_ref) in enumerate((
            (prev_g_ref, w_up_g_ref, w_dw_g_ref, b_dw_g_ref, tail_g_ref),
            (prev_v_ref, w_up_v_ref, w_dw_v_ref, b_dw_v_ref, tail_v_ref))):
        slot = part * n_chunks + j

        @pl.when(first)
        def _():
            ubuf_ref[part, :hist] = prev_ref[0]

        @pl.when(jnp.logical_not(first))
        def _():
            ubuf_ref[part, :hist] = carry_ref[slot]

        ubuf_ref[part, hist:] = _dot(h2_ref[...], w_up_ref[...])
        last_rows = ubuf_ref[part, tm:]
        carry_ref[slot] = last_rows
        tail_ref[0] = last_rows
        halves.append(b_dw_ref[...]
                      + w_dw_ref[0:1] * ubuf_ref[part, hist - 2 * shift:hist - 2 * shift + tm]
                      + w_dw_ref[1:2] * ubuf_ref[part, hist - shift:hist - shift + tm]
                      + w_dw_ref[2:3] * ubuf_ref[part, hist:])
    act = (halves[0] * jax.nn.sigmoid(halves[0]) * halves[1]).astype(_BF16)
    acc_ref[...] += _dot(act, w_down_ref[...])

    @pl.when(j == n_chunks - 1)
    def _():
        y_ref[...] = _rms(acc_ref[...], g_fin_ref[...]).reshape(gb, sb, d_model)


def _out_ffn(x, attn_n, conv_n, prev, wts, *, block, shift, tiles_per_seq):
    g_dim, s_dim, d_model = x.shape
    gb, sb = block
    n_s = s_dim // sb
    n_seq, hist, _ = prev.shape
    w_oa, w_oc, g_ffn, w_up, w_dw, b_dw, w_down, g_fin = wts
    fc, nj = FFN_CHUNK, FFN_CHUNKS
    row_spec = lambda w: pl.BlockSpec((gb, sb, w), lambda i, j: (i // n_s, i % n_s, 0))
    const = lambda w: pl.BlockSpec(w.shape, lambda i, j: (0,) * w.ndim, pipeline_mode=pl.Buffered(1))
    hist_spec = lambda part: pl.BlockSpec((1, hist, fc), lambda i, j: (i // tiles_per_seq, 0, part * nj + j))
    cols_spec = lambda rows, part: pl.BlockSpec((rows, fc), lambda i, j: (0, part * nj + j))
    return pl.pallas_call(
        functools.partial(_out_ffn_kernel, shift=shift, tiles_per_seq=tiles_per_seq),
        grid=((g_dim // gb) * n_s, nj),
        in_specs=[row_spec(d_model), row_spec(D_ATT), row_spec(D_CONV), hist_spec(0), hist_spec(1),
                  const(w_oa), const(w_oc), const(g_ffn),
                  cols_spec(d_model, 0), cols_spec(d_model, 1), cols_spec(SUBLANES, 0), cols_spec(SUBLANES, 1),
                  cols_spec(1, 0), cols_spec(1, 1), pl.BlockSpec((fc, d_model), lambda i, j: (j, 0)), const(g_fin)],
        out_specs=[row_spec(d_model), hist_spec(0), hist_spec(0)],
        out_shape=[jax.ShapeDtypeStruct(x.shape, _F32)] + [jax.ShapeDtypeStruct((n_seq, hist, D_FF), _F32)] * 2,
        scratch_shapes=[pltpu.VMEM((2, hist + gb * sb, fc), _F32), pltpu.VMEM((2 * nj, hist, fc), _F32),
                        pltpu.VMEM((gb * sb, d_model), _F32), pltpu.VMEM((gb * sb, d_model), _BF16)],
        compiler_params=pltpu.CompilerParams(dimension_semantics=("arbitrary", "arbitrary"),
                                             vmem_limit_bytes=VMEM_LIMIT),
        name="out_ffn",
    )(x, attn_n, conv_n, prev, prev, w_oa, w_oc, g_ffn, w_up, w_up, w_dw, w_dw, b_dw, b_dw, w_down, g_fin)


def _rope_tables(pos):
    half = D_ROPE // 2
    inv = ROPE_THETA ** (-(jnp.arange(half, dtype=_F32) * 2.0 / D_ROPE))
    ang = pos.astype(_F32)[:, None] * inv[None, :]
    cos, sin = jnp.cos(ang), jnp.sin(ang)
    pad = ((0, 0), (0, LANE - D_ROPE))
    return jnp.pad(jnp.concatenate([cos, cos], axis=-1), pad), jnp.pad(jnp.concatenate([-sin, sin], axis=-1), pad)


def _swap_halves(w):
    half = D_ROPE // 2
    return jnp.concatenate([w[..., half:], w[..., :half]], axis=-1)


def _pad_lanes(w):
    return jnp.pad(w, [(0, 0)] * (w.ndim - 1) + [(0, LANE - w.shape[-1])])


def kernel(x_prompt, x_sample, cache_kv_latent, state_conv, state_ffn_conv, page_table, g_attn_norm, w_in, g_q_norm, w_uq, g_kv_norm, w_uk, w_uv, w_dw, b_dw, g_conv_ln, b_conv_ln, g_attn_out, g_conv_out, w_o, g_ffn_norm, w_up, w_ffn_dw, b_ffn_dw, w_down, g_final):
    batch, seq, d_model = x_prompt.shape
    nb, steps, _ = x_sample.shape
    n_pages = page_table.shape[1]
    past_len = n_pages * PAGE_SIZE
    assert w_in.shape[0] == 1, "single-layer trunk"
    assert seq % Q_TILE == 0 and nb % CONV_ROWS == 0 and n_pages % PAGES_PER_STEP == 0 and steps <= PAGE_SIZE

    row = lambda v: v.reshape(1, -1)
    o1, o2, o3 = D_CQ, D_CQ + D_C, D_CQ + D_C + D_ROPE
    wi = w_in[0]
    w_in_ext = jnp.concatenate([wi[:, :o2], _pad_lanes(wi[:, o2:o3]), _pad_lanes(_swap_halves(wi[:, o2:o3])),
                                wi[:, o3:]], axis=1).astype(_BF16)
    wq = w_uq[0]
    w_qn = wq[:, :, :D_NOPE].reshape(D_CQ, N_HEADS * D_NOPE).astype(_BF16)
    w_qr = _pad_lanes(wq[:, :, D_NOPE:]).reshape(D_CQ, N_HEADS * LANE).astype(_BF16)
    w_qs = _pad_lanes(_swap_halves(wq[:, :, D_NOPE:])).reshape(D_CQ, N_HEADS * LANE).astype(_BF16)
    wk = jnp.transpose(w_uk[0], (1, 2, 0))
    sel = (jnp.arange(HEAD_GROUP)[None, :] == (jnp.arange(N_HEADS) % HEAD_GROUP)[:, None]).astype(_F32)
    w_uk_z = (sel[:, :, None, None] * wk[:, None]).reshape(N_HEADS, HEAD_GROUP * D_NOPE, D_C).astype(_BF16)
    wv = jnp.transpose(w_uv[0], (1, 0, 2))
    sel2 = (jnp.arange(2)[None, :] == (jnp.arange(N_HEADS) % 2)[:, None]).astype(_F32)
    w_uv_z = (wv[:, :, None, :] * sel2[:, None, :, None]).reshape(N_HEADS, D_C, 2 * D_V).astype(_BF16)
    in_wts = (row(g_attn_norm[0]), w_in_ext, row(g_q_norm[0]), w_qn, w_qr, w_qs, row(g_kv_norm[0]), w_uk_z)
    conv_wts = (jnp.pad(w_dw[0], ((0, 1), (0, 0))), row(b_dw[0]), row(g_conv_ln[0]), row(b_conv_ln[0]),
                row(g_conv_out[0]))
    g_ao = row(g_attn_out[0])
    wo = w_o[0].astype(_BF16)
    w_fdw = jnp.pad(w_ffn_dw[0], ((0, SUBLANES - FFN_CONV_W), (0, 0)))
    ffn_wts = (wo[:D_ATT], wo[D_ATT:], row(g_ffn_norm[0]), w_up[0].astype(_BF16), w_fdw, row(b_ffn_dw[0]),
               w_down[0].astype(_BF16), row(g_final))

    rows_p = batch * seq
    tm = min(512, seq)
    xp = x_prompt.reshape(rows_p, d_model)
    cos_p, sin_p = _rope_tables(jnp.arange(seq, dtype=jnp.int32))
    q_p, kv_p, kvb_p, glu_p = _inproj(xp, cos_p, sin_p, in_wts, tm=tm, tq=Q_TILE)
    conv_p = _conv_prompt(glu_p, conv_wts, batch=batch, seq=seq, tc=tm)
    attn_p = _attn_prompt(q_p, kvb_p, w_uv_z, g_ao, batch=batch, seq=seq, tk=min(512, seq))
    hist_p = SUBLANES
    prev_p = jnp.zeros((batch, hist_p, 2 * D_FF), _F32)
    tiled = lambda v: v.reshape(rows_p // tm, tm, v.shape[-1])
    y_p, tail_g, tail_v = _out_ffn(tiled(xp), tiled(attn_p), tiled(conv_p), prev_p, ffn_wts, block=(1, tm), shift=1,
                                   tiles_per_seq=seq // tm)
    tail_p = jnp.concatenate([tail_g, tail_v], axis=-1)
    y_prompt = y_p.reshape(batch, seq, d_model)
    new_kv_prompt = kv_p.reshape(1, batch, seq, D_QK)
    new_conv_prompt = glu_p.reshape(batch, seq, D_CONV)[None, :, seq - (CONV_W - 1):]
    new_ffn_prompt = tail_p[None, :, hist_p - (FFN_CONV_W - 1):]

    rows_s = nb * steps
    xs = jnp.transpose(x_sample, (1, 0, 2)).reshape(rows_s, d_model)
    pos_s = past_len + jnp.arange(rows_s, dtype=jnp.int32) // nb
    cos_s, sin_s = _rope_tables(pos_s)
    tq_s = min(Q_TILE, nb)
    q_s, kv_s, kvb_s, glu_s = _inproj(xs, cos_s, sin_s, in_wts, tm=rows_s, tq=tq_s)
    q_s = q_s.reshape(steps, nb // tq_s, N_HEADS, tq_s, D_QK_PAD)
    q_s = jnp.transpose(q_s, (1, 3, 2, 0, 4)).reshape(nb, N_HEADS * steps, D_QK_PAD)
    kvn_s = jnp.transpose(kvb_s.reshape(steps, nb, D_QK_PAD), (1, 0, 2))
    ext = jnp.concatenate([jnp.transpose(state_conv[0], (1, 0, 2)), glu_s.reshape(steps, nb, D_CONV)], axis=0)
    conv_s = _conv_sample(ext, conv_wts, steps=steps).reshape(rows_s, D_CONV)
    attn_s = _attn_sample(page_table, q_s, kvn_s, cache_kv_latent[0], w_uv_z, g_ao)
    attn_s = jnp.transpose(attn_s, (1, 0, 2)).reshape(rows_s, D_ATT)
    sb = min(SAMPLE_FFN_SEQS, nb)
    n_tiles = nb // sb
    hist_s = (FFN_CONV_W - 1) * sb
    prev_s = jnp.transpose(state_ffn_conv[0].reshape(n_tiles, sb, FFN_CONV_W - 1, 2 * D_FF), (0, 2, 1, 3))
    prev_s = prev_s.reshape(n_tiles, hist_s, 2 * D_FF)
    steps3 = lambda v: v.reshape(steps, nb, v.shape[-1])
    y_s, tail_g, tail_v = _out_ffn(steps3(xs), steps3(attn_s), steps3(conv_s), prev_s, ffn_wts, block=(steps, sb),
                                   shift=sb, tiles_per_seq=1)
    tail_s = jnp.concatenate([tail_g, tail_v], axis=-1)
    y_sample = jnp.transpose(y_s, (1, 0, 2))
    new_kv_sample = jnp.transpose(kv_s.reshape(steps, nb, D_QK), (1, 0, 2))[None]
    new_conv_sample = jnp.transpose(ext[steps:], (1, 0, 2))[None]
    tail_s = tail_s.reshape(n_tiles, FFN_CONV_W - 1, sb, 2 * D_FF)
    new_ffn_sample = jnp.transpose(tail_s, (0, 2, 1, 3)).reshape(1, nb, FFN_CONV_W - 1, 2 * D_FF)
    return (y_prompt, y_sample, new_kv_prompt, new_conv_prompt, new_ffn_prompt,
            new_kv_sample, new_conv_sample, new_ffn_sample)
```

```python
import functools

import jax
import jax.numpy as jnp
from jax import lax
from jax.experimental import pallas as pl
from jax.experimental.pallas import tpu as pltpu

N_HEADS = 8
D_NOPE = 64
D_ROPE = 32
D_V = 64
D_C = 256
D_CQ = 384
D_ATT = N_HEADS * D_V
D_CONV = 512
CONV_W = 31
D_FF = 2816
FFN_CONV_W = 3
ROPE_THETA = 10000.0
PAGE_SIZE = 128
EPS = 1e-6
ATTN_SCALE = (D_NOPE + D_ROPE) ** -0.5

LANE = 128
SUBLANES = 8
D_QK = D_C + D_ROPE
D_QK_PAD = D_C + LANE
Q_TILE = 128
HEAD_GROUP = 256 // D_NOPE
FFN_CHUNKS = 2
FFN_CHUNK = D_FF // FFN_CHUNKS
PAGES_PER_STEP = 32
SAMPLE_FFN_SEQS = 32
VMEM_LIMIT = 56 * 1024 * 1024

_BF16 = jnp.bfloat16
_F32 = jnp.float32


def _rms(x, g):
    return x * lax.rsqrt(jnp.mean(x * x, axis=-1, keepdims=True) + EPS) * g


def _dot(a, b):
    return jnp.dot(a, b, preferred_element_type=_F32)


def _dot_nt(a, b):
    return lax.dot_general(a, b, (((1,), (1,)), ((), ())), preferred_element_type=_F32)


def _const_spec(shape):
    nd = len(shape)
    return pl.BlockSpec(shape, lambda *_: (0,) * nd, pipeline_mode=pl.Buffered(1))


def _inproj_kernel(x_ref, cos_ref, sin_ref, g_in_ref, w_in_ref, g_q_ref, w_qn_ref, w_qr_ref,
                   w_qs_ref, g_kv_ref, w_uk_ref, q_ref, kv_ref, kvb_ref, glu_ref, *, tq):
    tm = x_ref.shape[0]
    cos = cos_ref[...]
    sin = sin_ref[...]
    h = _rms(x_ref[...], g_in_ref[...]).astype(_BF16)
    proj = _dot(h, w_in_ref[...])
    o_kv, o_kr, o_ks, o_ga, o_gb = D_CQ, D_CQ + D_C, D_CQ + D_C + LANE, D_CQ + D_C + 2 * LANE, D_CQ + D_C + 2 * LANE + D_CONV
    cq = proj[:, :o_kv]
    ckv = proj[:, o_kv:o_kr]
    glu_ref[...] = proj[:, o_ga:o_gb] * jax.nn.sigmoid(proj[:, o_gb:])
    ckv_n = _rms(ckv, g_kv_ref[...])
    kr = proj[:, o_kr:o_ks] * cos + proj[:, o_ks:o_ga] * sin
    kv_ref[:, :D_C] = ckv_n
    kv_ref[:, D_C:] = kr[:, :D_ROPE]
    kvb_ref[:, :D_C] = ckv_n.astype(_BF16)
    kvb_ref[:, D_C:] = kr.astype(_BF16)
    cqn = _rms(cq, g_q_ref[...]).astype(_BF16)
    qn = _dot(cqn, w_qn_ref[...]).astype(_BF16)
    qr = _dot(cqn, w_qr_ref[...])
    qs = _dot(cqn, w_qs_ref[...])
    for hd in range(N_HEADS):
        grp = hd // HEAD_GROUP
        ql = _dot(qn[:, grp * 256:(grp + 1) * 256], w_uk_ref[hd]).astype(_BF16)
        qrh = (qr[:, hd * LANE:(hd + 1) * LANE] * cos + qs[:, hd * LANE:(hd + 1) * LANE] * sin).astype(_BF16)
        for s in range(tm // tq):
            q_ref[s, hd, :, :D_C] = ql[s * tq:(s + 1) * tq]
            q_ref[s, hd, :, D_C:] = qrh[s * tq:(s + 1) * tq]


def _inproj(x, cos_tab, sin_tab, wts, *, tm, tq):
    rows = x.shape[0]
    n_tab = cos_tab.shape[0] // tm
    g_in, w_in, g_q, w_qn, w_qr, w_qs, g_kv, w_uk = wts
    row_spec = lambda w: pl.BlockSpec((tm, w), lambda i: (i, 0))
    tab_spec = pl.BlockSpec((tm, LANE), lambda i: (i % n_tab, 0))
    return pl.pallas_call(
        functools.partial(_inproj_kernel, tq=tq),
        grid=(rows // tm,),
        in_specs=[row_spec(x.shape[1]), tab_spec, tab_spec] + [_const_spec(w.shape) for w in wts],
        out_specs=[pl.BlockSpec((tm // tq, N_HEADS, tq, D_QK_PAD), lambda i: (i, 0, 0, 0)),
                   row_spec(D_QK), row_spec(D_QK_PAD), row_spec(D_CONV)],
        out_shape=[jax.ShapeDtypeStruct((rows // tq, N_HEADS, tq, D_QK_PAD), _BF16),
                   jax.ShapeDtypeStruct((rows, D_QK), _F32),
                   jax.ShapeDtypeStruct((rows, D_QK_PAD), _BF16),
                   jax.ShapeDtypeStruct((rows, D_CONV), _F32)],
        compiler_params=pltpu.CompilerParams(dimension_semantics=("parallel",), vmem_limit_bytes=VMEM_LIMIT),
        name="inproj",
    )(x, cos_tab, sin_tab, *wts)


CONV_HALO = 32
CONV_ROWS = 16
CONV_UNROLL = 4


def _conv_post(y, g_ln, b_ln, g_out):
    mu = jnp.mean(y, axis=-1, keepdims=True)
    yc = y - mu
    z = yc * lax.rsqrt(jnp.mean(yc * yc, axis=-1, keepdims=True) + EPS) * g_ln + b_ln
    return _rms(z * jax.nn.sigmoid(z), g_out).astype(_BF16)


def _conv_prompt_kernel(cur_ref, halo_ref, w_ref, b_ref, g_ln_ref, b_ln_ref, g_out_ref, o_ref, ext_ref, sh_ref):
    tc = cur_ref.shape[1]
    first = pl.program_id(1) == 0
    ext_ref[:CONV_HALO] = jnp.where(first, 0.0, halo_ref[0])
    ext_ref[CONV_HALO:CONV_HALO + tc] = cur_ref[0]
    ext_ref[CONV_HALO + tc:] = jnp.zeros((SUBLANES, D_CONV), _F32)
    off = CONV_HALO - (CONV_W - 1)
    for r in range(SUBLANES):
        sh_ref[r] = ext_ref[r:r + sh_ref.shape[1], :]

    def body(it, carry):
        for u in range(CONV_UNROLL):
            base = pl.multiple_of((it * CONV_UNROLL + u) * CONV_ROWS, CONV_ROWS)
            acc = jnp.broadcast_to(b_ref[...], (CONV_ROWS, D_CONV))
            for k in range(CONV_W):
                r = (off + k) % SUBLANES
                acc = acc + w_ref[k:k + 1, :] * sh_ref[r, pl.ds(base + (off + k - r), CONV_ROWS), :]
            o_ref[pl.ds(base, CONV_ROWS), :] = _conv_post(acc, g_ln_ref[...], b_ln_ref[...], g_out_ref[...])
        return carry

    lax.fori_loop(0, tc // (CONV_ROWS * CONV_UNROLL), body, 0)


def _conv_prompt(glu, wts, *, batch, seq, tc):
    glu3 = glu.reshape(batch, seq, D_CONV)
    hb = tc // CONV_HALO
    return pl.pallas_call(
        _conv_prompt_kernel,
        grid=(batch, seq // tc),
        in_specs=[pl.BlockSpec((1, tc, D_CONV), lambda b, i: (b, i, 0)),
                  pl.BlockSpec((1, CONV_HALO, D_CONV), lambda b, i: (b, jnp.maximum(i * hb - 1, 0), 0))]
                 + [_const_spec(w.shape) for w in wts],
        out_specs=pl.BlockSpec((tc, D_CONV), lambda b, i: (b * (seq // tc) + i, 0)),
        out_shape=jax.ShapeDtypeStruct((batch * seq, D_CONV), _BF16),
        scratch_shapes=[pltpu.VMEM((CONV_HALO + tc + SUBLANES, D_CONV), _F32),
                        pltpu.VMEM((SUBLANES, CONV_HALO + tc, D_CONV), _F32)],
        compiler_params=pltpu.CompilerParams(dimension_semantics=("parallel", "parallel"), vmem_limit_bytes=VMEM_LIMIT),
        name="conv_prompt",
    )(glu3, glu3, *wts)


def _conv_sample_kernel(ext_ref, w_ref, b_ref, g_ln_ref, b_ln_ref, g_out_ref, o_ref):
    steps, nb = o_ref.shape[0], o_ref.shape[1]
    chunks = nb // CONV_ROWS

    def body(it, carry):
        t = it // chunks
        base = pl.multiple_of((it % chunks) * CONV_ROWS, CONV_ROWS)
        acc = jnp.broadcast_to(b_ref[...], (CONV_ROWS, D_CONV))
        for k in range(CONV_W):
            acc = acc + w_ref[k:k + 1, :] * ext_ref[t + k, pl.ds(base, CONV_ROWS), :]
        o_ref[t, pl.ds(base, CONV_ROWS), :] = _conv_post(acc, g_ln_ref[...], b_ln_ref[...], g_out_ref[...])
        return carry

    lax.fori_loop(0, steps * chunks, body, 0)


def _conv_sample(ext, wts, *, steps):
    nb = ext.shape[1]
    return pl.pallas_call(
        _conv_sample_kernel,
        out_shape=jax.ShapeDtypeStruct((steps, nb, D_CONV), _BF16),
        compiler_params=pltpu.CompilerParams(vmem_limit_bytes=VMEM_LIMIT),
        name="conv_sample",
    )(ext, *wts)


def _softmax_step(s, m_prev, l_prev, acc_prev, v):
    m_new = jnp.maximum(m_prev, jnp.max(s, axis=-1, keepdims=True))
    alpha = jnp.exp(m_prev - m_new)
    p = jnp.exp(s - m_new)
    l_new = alpha * l_prev + jnp.sum(p, axis=-1, keepdims=True)
    acc_new = alpha * acc_prev + _dot(p.astype(_BF16), v)
    return m_new, l_new, acc_new


def _attn_out(o, rows_per_head, w_uv_ref, g_ref):
    parts = []
    for pr in range(N_HEADS // 2):
        acc = None
        for hd in (2 * pr, 2 * pr + 1):
            oh = o[hd * rows_per_head:(hd + 1) * rows_per_head].astype(_BF16)
            term = _dot(oh, w_uv_ref[hd])
            acc = term if acc is None else acc + term
        parts.append(acc)
    attn = jnp.concatenate(parts, axis=-1)
    return _rms(attn, g_ref[...]).astype(_BF16)


def _attn_prompt_kernel(q_ref, kv_ref, w_uv_ref, g_ref, o_ref, s_ref, s_next_ref, m_ref, l_ref, acc_ref, *, tk):
    tq = q_ref.shape[2]
    rows = N_HEADS * tq
    qi = pl.program_id(1)
    m_ref[...] = jnp.full(m_ref.shape, -jnp.inf, _F32)
    l_ref[...] = jnp.zeros(l_ref.shape, _F32)
    acc_ref[...] = jnp.zeros(acc_ref.shape, _F32)
    n_full = (qi * tq) // tk

    def keys(c):
        return kv_ref[0, pl.ds(pl.multiple_of(c * tk, tk), tk), :]

    def scores(c):
        return _dot_nt(q_ref[0].reshape(rows, D_QK_PAD), keys(c))

    def softmax_pv(s, v):
        m, l, acc = _softmax_step(s * ATTN_SCALE, m_ref[...], l_ref[...], acc_ref[...], v)
        m_ref[...] = m
        l_ref[...] = l
        acc_ref[...] = acc

    s_ref[...] = scores(0)

    def body(c, carry):
        s_next_ref[...] = scores(c + 1)
        softmax_pv(s_ref[...], keys(c)[:, :D_C])
        s_ref[...] = s_next_ref[...]
        return carry

    lax.fori_loop(0, n_full, body, 0)
    for r in range(tk // tq):
        @pl.when(qi % (tk // tq) == r)
        def _():
            width = (r + 1) * tq
            s = s_ref[:, :width]
            t_idx = r * tq + lax.broadcasted_iota(jnp.int32, (rows, width), 0) % tq
            k_idx = lax.broadcasted_iota(jnp.int32, (rows, width), 1)
            s = jnp.where(k_idx <= t_idx, s, -jnp.inf)
            softmax_pv(s, keys(n_full)[:width, :D_C])

    o_ref[...] = _attn_out(acc_ref[...] / l_ref[...], tq, w_uv_ref, g_ref)


def _attn_prompt(q, kvb, w_uv, g, *, batch, seq, tk):
    tq = q.shape[2]
    nq = seq // tq
    rows = N_HEADS * tq
    return pl.pallas_call(
        functools.partial(_attn_prompt_kernel, tk=tk),
        grid=(batch, nq),
        in_specs=[pl.BlockSpec((1, N_HEADS, tq, D_QK_PAD), lambda b, i: (b * nq + i, 0, 0, 0)),
                  pl.BlockSpec((1, seq, D_QK_PAD), lambda b, i: (b, 0, 0)),
                  _const_spec(w_uv.shape), _const_spec(g.shape)],
        out_specs=pl.BlockSpec((tq, D_ATT), lambda b, i: (b * nq + i, 0)),
        out_shape=jax.ShapeDtypeStruct((batch * seq, D_ATT), _BF16),
        scratch_shapes=[pltpu.VMEM((rows, tk), _F32), pltpu.VMEM((rows, tk), _F32),
                        pltpu.VMEM((rows, 1), _F32), pltpu.VMEM((rows, 1), _F32),
                        pltpu.VMEM((rows, D_C), _F32)],
        compiler_params=pltpu.CompilerParams(dimension_semantics=("parallel", "parallel"), vmem_limit_bytes=VMEM_LIMIT),
        name="attn_prompt",
    )(q, kvb.reshape(batch, seq, D_QK_PAD), w_uv, g)


def _attn_sample_kernel(pt_ref, q_ref, kvn_ref, w_uv_ref, g_ref, *rest, steps):
    page_refs = rest[:PAGES_PER_STEP]
    o_ref, kt_ref, kn_ref, m_ref, l_ref, acc_ref = rest[PAGES_PER_STEP:]
    c = pl.program_id(1)
    rows = q_ref.shape[1]

    @pl.when(c == 0)
    def _():
        kt_ref[D_QK:, :] = jnp.zeros((D_QK_PAD - D_QK, kt_ref.shape[1]), _BF16)
        m_ref[...] = jnp.full(m_ref.shape, -jnp.inf, _F32)
        l_ref[...] = jnp.zeros(l_ref.shape, _F32)
        acc_ref[...] = jnp.zeros(acc_ref.shape, _F32)

    for p, page_ref in enumerate(page_refs):
        kt_ref[:D_QK, p * PAGE_SIZE:(p + 1) * PAGE_SIZE] = page_ref[0].astype(_BF16)
    s = _dot(q_ref[0], kt_ref[...]) * ATTN_SCALE
    m_prev = m_ref[...]
    m_new = jnp.maximum(m_prev, jnp.max(s, axis=-1, keepdims=True))
    alpha = jnp.exp(m_prev - m_new)
    p_un = jnp.exp(s - m_new)
    m_ref[...] = m_new
    l_ref[...] = alpha * l_ref[...] + jnp.sum(p_un, axis=-1, keepdims=True)
    acc_ref[...] = alpha * acc_ref[...] + _dot_nt(p_un.astype(_BF16), kt_ref[:D_C, :])

    @pl.when(c == pl.num_programs(1) - 1)
    def _():
        kn_ref[...] = jnp.zeros(kn_ref.shape, _BF16)
        kn_ref[:steps, :] = kvn_ref[0]
        kn = kn_ref[...]
        sn = _dot_nt(q_ref[0], kn) * ATTN_SCALE
        t_pos = lax.broadcasted_iota(jnp.int32, (rows, PAGE_SIZE), 0) % steps
        j_pos = lax.broadcasted_iota(jnp.int32, (rows, PAGE_SIZE), 1)
        sn = jnp.where(j_pos <= t_pos, sn, -jnp.inf)
        _, l2, acc2 = _softmax_step(sn, m_ref[...], l_ref[...], acc_ref[...], kn[:, :D_C])
        o_ref[0] = _attn_out(acc2 / l2, steps, w_uv_ref, g_ref)


def _attn_sample(page_table, q, kv_new, cache_t, w_uv, g):
    nb, rows, _ = q.shape
    steps = kv_new.shape[1]
    n_pages = page_table.shape[1]
    n_chunks = n_pages // PAGES_PER_STEP

    def page_spec(p):
        return pl.BlockSpec((1, D_QK, PAGE_SIZE),
                            lambda b, c, pt: (pt[b * n_pages + c * PAGES_PER_STEP + p], 0, 0))

    const = lambda shape: pl.BlockSpec(shape, lambda b, c, pt: (0,) * len(shape), pipeline_mode=pl.Buffered(1))
    return pl.pallas_call(
        functools.partial(_attn_sample_kernel, steps=steps),
        grid_spec=pltpu.PrefetchScalarGridSpec(
            num_scalar_prefetch=1,
            grid=(nb, n_chunks),
            in_specs=[pl.BlockSpec((1, rows, D_QK_PAD), lambda b, c, pt: (b, 0, 0)),
                      pl.BlockSpec((1, steps, D_QK_PAD), lambda b, c, pt: (b, 0, 0)),
                      const(w_uv.shape), const(g.shape)]
                     + [page_spec(p) for p in range(PAGES_PER_STEP)],
            out_specs=pl.BlockSpec((1, steps, D_ATT), lambda b, c, pt: (b, 0, 0)),
            scratch_shapes=[pltpu.VMEM((D_QK_PAD, PAGES_PER_STEP * PAGE_SIZE), _BF16),
                            pltpu.VMEM((PAGE_SIZE, D_QK_PAD), _BF16),
                            pltpu.VMEM((rows, 1), _F32), pltpu.VMEM((rows, 1), _F32),
                            pltpu.VMEM((rows, D_C), _F32)]),
        out_shape=jax.ShapeDtypeStruct((nb, steps, D_ATT), _BF16),
        compiler_params=pltpu.CompilerParams(dimension_semantics=("parallel", "arbitrary"), vmem_limit_bytes=VMEM_LIMIT),
        name="attn_sample",
    )(page_table.reshape(-1), q, kv_new, w_uv, g, *([cache_t] * PAGES_PER_STEP))


def _out_ffn_kernel(x_ref, attn_ref, conv_ref, prev_g_ref, prev_v_ref, w_oa_ref, w_oc_ref, g_ffn_ref,
                    w_up_g_ref, w_up_v_ref, w_dw_g_ref, w_dw_v_ref, b_dw_g_ref, b_dw_v_ref, w_down_ref, g_fin_ref,
                    y_ref, tail_g_ref, tail_v_ref, ubuf_ref, carry_ref, acc_ref, h2_ref, *, shift, tiles_per_seq):
    gb, sb, d_model = x_ref.shape
    tm = gb * sb
    hist = prev_g_ref.shape[1]
    j = pl.program_id(1)
    n_chunks = pl.num_programs(1)
    first = pl.program_id(0) % tiles_per_seq == 0

    @pl.when(j == 0)
    def _():
        flat = lambda ref: ref[...].reshape(tm, ref.shape[2])
        acc_ref[...] = flat(x_ref) + _dot(flat(attn_ref), w_oa_ref[...]) + _dot(flat(conv_ref), w_oc_ref[...])
        h2_ref[...] = _rms(acc_ref[...], g_ffn_ref[...]).astype(_BF16)

    halves = []
    for part, (prev_ref, w_up_ref, w_dw_ref, b_dw_ref, tail_ref) in enumerate((
            (prev_g_ref, w_up_g_ref, w_dw_g_ref, b_dw_g_ref, tail_g_ref),
            (prev_v_ref, w_up_v_ref, w_dw_v_ref, b_dw_v_ref, tail_v_ref))):
        slot = part * n_chunks + j

        @pl.when(first)
        def _():
            ubuf_ref[part, :hist] = prev_ref[0]

        @pl.when(jnp.logical_not(first))
        def _():
            ubuf_ref[part, :hist] = carry_ref[slot]

        ubuf_ref[part, hist:] = _dot(h2_ref[...], w_up_ref[...])
        last_rows = ubuf_ref[part, tm:]
        carry_ref[slot] = last_rows
        tail_ref[0] = last_rows
        halves.append(b_dw_ref[...]
                      + w_dw_ref[0:1] * ubuf_ref[part, hist - 2 * shift:hist - 2 * shift + tm]
                      + w_dw_ref[1:2] * ubuf_ref[part, hist - shift:hist - shift + tm]
                      + w_dw_ref[2:3] * ubuf_ref[part, hist:])
    act = (halves[0] * jax.nn.sigmoid(halves[0]) * halves[1]).astype(_BF16)
    acc_ref[...] += _dot(act, w_down_ref[...])

    @pl.when(j == n_chunks - 1)
    def _():
        y_ref[...] = _rms(acc_ref[...], g_fin_ref[...]).reshape(gb, sb, d_model)


def _out_ffn(x, attn_n, conv_n, prev, wts, *, block, shift, tiles_per_seq):
    g_dim, s_dim, d_model = x.shape
    gb, sb = block
    n_s = s_dim // sb
    n_seq, hist, _ = prev.shape
    w_oa, w_oc, g_ffn, w_up, w_dw, b_dw, w_down, g_fin = wts
    fc, nj = FFN_CHUNK, FFN_CHUNKS
    row_spec = lambda w: pl.BlockSpec((gb, sb, w), lambda i, j: (i // n_s, i % n_s, 0))
    const = lambda w: pl.BlockSpec(w.shape, lambda i, j: (0,) * w.ndim, pipeline_mode=pl.Buffered(1))
    hist_spec = lambda part: pl.BlockSpec((1, hist, fc), lambda i, j: (i // tiles_per_seq, 0, part * nj + j))
    cols_spec = lambda rows, part: pl.BlockSpec((rows, fc), lambda i, j: (0, part * nj + j))
    tail_spec = pl.BlockSpec((1, hist, fc), lambda i, j: (i, 0, j))
    n_tiles = (g_dim // gb) * n_s
    return pl.pallas_call(
        functools.partial(_out_ffn_kernel, shift=shift, tiles_per_seq=tiles_per_seq),
        grid=(n_tiles, nj),
        in_specs=[row_spec(d_model), row_spec(D_ATT), row_spec(D_CONV), hist_spec(0), hist_spec(1),
                  const(w_oa), const(w_oc), const(g_ffn),
                  cols_spec(d_model, 0), cols_spec(d_model, 1), cols_spec(SUBLANES, 0), cols_spec(SUBLANES, 1),
                  cols_spec(1, 0), cols_spec(1, 1), pl.BlockSpec((fc, d_model), lambda i, j: (j, 0)), const(g_fin)],
        out_specs=[row_spec(d_model), tail_spec, tail_spec],
        out_shape=[jax.ShapeDtypeStruct(x.shape, _F32)] + [jax.ShapeDtypeStruct((n_tiles, hist, D_FF), _F32)] * 2,
        scratch_shapes=[pltpu.VMEM((2, hist + gb * sb, fc), _F32), pltpu.VMEM((2 * nj, hist, fc), _F32),
                        pltpu.VMEM((gb * sb, d_model), _F32), pltpu.VMEM((gb * sb, d_model), _BF16)],
        compiler_params=pltpu.CompilerParams(dimension_semantics=("arbitrary", "arbitrary"),
                                             vmem_limit_bytes=VMEM_LIMIT),
        name="out_ffn",
    )(x, attn_n, conv_n, prev, prev, w_oa, w_oc, g_ffn, w_up, w_up, w_dw, w_dw, b_dw, b_dw, w_down, g_fin)


def _rope_tables(pos):
    half = D_ROPE // 2
    inv = ROPE_THETA ** (-(jnp.arange(half, dtype=_F32) * 2.0 / D_ROPE))
    ang = pos.astype(_F32)[:, None] * inv[None, :]
    cos, sin = jnp.cos(ang), jnp.sin(ang)
    pad = ((0, 0), (0, LANE - D_ROPE))
    return jnp.pad(jnp.concatenate([cos, cos], axis=-1), pad), jnp.pad(jnp.concatenate([-sin, sin], axis=-1), pad)


def _swap_halves(w):
    half = D_ROPE // 2
    return jnp.concatenate([w[..., half:], w[..., :half]], axis=-1)


def _pad_lanes(w):
    return jnp.pad(w, [(0, 0)] * (w.ndim - 1) + [(0, LANE - w.shape[-1])])


def kernel(x_prompt, x_sample, cache_kv_latent, state_conv, state_ffn_conv, page_table, g_attn_norm, w_in, g_q_norm, w_uq, g_kv_norm, w_uk, w_uv, w_dw, b_dw, g_conv_ln, b_conv_ln, g_attn_out, g_conv_out, w_o, g_ffn_norm, w_up, w_ffn_dw, b_ffn_dw, w_down, g_final):
    batch, seq, d_model = x_prompt.shape
    nb, steps, _ = x_sample.shape
    n_pages = page_table.shape[1]
    past_len = n_pages * PAGE_SIZE
    assert w_in.shape[0] == 1, "single-layer trunk"
    assert seq % Q_TILE == 0 and nb % CONV_ROWS == 0 and n_pages % PAGES_PER_STEP == 0 and steps <= PAGE_SIZE

    row = lambda v: v.reshape(1, -1)
    o1, o2, o3 = D_CQ, D_CQ + D_C, D_CQ + D_C + D_ROPE
    wi = w_in[0]
    w_in_ext = jnp.concatenate([wi[:, :o2], _pad_lanes(wi[:, o2:o3]), _pad_lanes(_swap_halves(wi[:, o2:o3])),
                                wi[:, o3:]], axis=1).astype(_BF16)
    wq = w_uq[0]
    w_qn = wq[:, :, :D_NOPE].reshape(D_CQ, N_HEADS * D_NOPE).astype(_BF16)
    w_qr = _pad_lanes(wq[:, :, D_NOPE:]).reshape(D_CQ, N_HEADS * LANE).astype(_BF16)
    w_qs = _pad_lanes(_swap_halves(wq[:, :, D_NOPE:])).reshape(D_CQ, N_HEADS * LANE).astype(_BF16)
    wk = jnp.transpose(w_uk[0], (1, 2, 0))
    sel = (jnp.arange(HEAD_GROUP)[None, :] == (jnp.arange(N_HEADS) % HEAD_GROUP)[:, None]).astype(_F32)
    w_uk_z = (sel[:, :, None, None] * wk[:, None]).reshape(N_HEADS, HEAD_GROUP * D_NOPE, D_C).astype(_BF16)
    wv = jnp.transpose(w_uv[0], (1, 0, 2))
    sel2 = (jnp.arange(2)[None, :] == (jnp.arange(N_HEADS) % 2)[:, None]).astype(_F32)
    w_uv_z = (wv[:, :, None, :] * sel2[:, None, :, None]).reshape(N_HEADS, D_C, 2 * D_V).astype(_BF16)
    in_wts = (row(g_attn_norm[0]), w_in_ext, row(g_q_norm[0]), w_qn, w_qr, w_qs, row(g_kv_norm[0]), w_uk_z)
    conv_wts = (jnp.pad(w_dw[0], ((0, 1), (0, 0))), row(b_dw[0]), row(g_conv_ln[0]), row(b_conv_ln[0]),
                row(g_conv_out[0]))
    g_ao = row(g_attn_out[0])
    wo = w_o[0].astype(_BF16)
    w_fdw = jnp.pad(w_ffn_dw[0], ((0, SUBLANES - FFN_CONV_W), (0, 0)))
    ffn_wts = (wo[:D_ATT], wo[D_ATT:], row(g_ffn_norm[0]), w_up[0].astype(_BF16), w_fdw, row(b_ffn_dw[0]),
               w_down[0].astype(_BF16), row(g_final))

    rows_p = batch * seq
    tm = min(512, seq)
    xp = x_prompt.reshape(rows_p, d_model)
    cos_p, sin_p = _rope_tables(jnp.arange(seq, dtype=jnp.int32))
    q_p, kv_p, kvb_p, glu_p = _inproj(xp, cos_p, sin_p, in_wts, tm=tm, tq=Q_TILE)
    conv_p = _conv_prompt(glu_p, conv_wts, batch=batch, seq=seq, tc=tm)
    attn_p = _attn_prompt(q_p, kvb_p, w_uv_z, g_ao, batch=batch, seq=seq, tk=min(512, seq))
    hist_p = SUBLANES
    prev_p = jnp.zeros((batch, hist_p, 2 * D_FF), _F32)
    tiled = lambda v: v.reshape(rows_p // tm, tm, v.shape[-1])
    y_p, tail_g, tail_v = _out_ffn(tiled(xp), tiled(attn_p), tiled(conv_p), prev_p, ffn_wts, block=(1, tm), shift=1,
                                   tiles_per_seq=seq // tm)
    last = seq // tm - 1
    tail_p = jnp.concatenate([tail_g[last::seq // tm], tail_v[last::seq // tm]], axis=-1)
    y_prompt = y_p.reshape(batch, seq, d_model)
    new_kv_prompt = kv_p.reshape(1, batch, seq, D_QK)
    new_conv_prompt = glu_p.reshape(batch, seq, D_CONV)[None, :, seq - (CONV_W - 1):]
    new_ffn_prompt = tail_p[None, :, hist_p - (FFN_CONV_W - 1):]

    rows_s = nb * steps
    xs = jnp.transpose(x_sample, (1, 0, 2)).reshape(rows_s, d_model)
    pos_s = past_len + jnp.arange(rows_s, dtype=jnp.int32) // nb
    cos_s, sin_s = _rope_tables(pos_s)
    tq_s = min(Q_TILE, nb)
    q_s, kv_s, kvb_s, glu_s = _inproj(xs, cos_s, sin_s, in_wts, tm=rows_s, tq=tq_s)
    q_s = q_s.reshape(steps, nb // tq_s, N_HEADS, tq_s, D_QK_PAD)
    q_s = jnp.transpose(q_s, (1, 3, 2, 0, 4)).reshape(nb, N_HEADS * steps, D_QK_PAD)
    kvn_s = jnp.transpose(kvb_s.reshape(steps, nb, D_QK_PAD), (1, 0, 2))
    ext = jnp.concatenate([jnp.transpose(state_conv[0], (1, 0, 2)), glu_s.reshape(steps, nb, D_CONV)], axis=0)
    conv_s = _conv_sample(ext, conv_wts, steps=steps).reshape(rows_s, D_CONV)
    attn_s = _attn_sample(page_table, q_s, kvn_s, jnp.swapaxes(cache_kv_latent[0], 1, 2), w_uv_z, g_ao)
    attn_s = jnp.transpose(attn_s, (1, 0, 2)).reshape(rows_s, D_ATT)
    sb = min(SAMPLE_FFN_SEQS, nb)
    n_tiles = nb // sb
    hist_s = (FFN_CONV_W - 1) * sb
    prev_s = jnp.transpose(state_ffn_conv[0].reshape(n_tiles, sb, FFN_CONV_W - 1, 2 * D_FF), (0, 2, 1, 3))
    prev_s = prev_s.reshape(n_tiles, hist_s, 2 * D_FF)
    steps3 = lambda v: v.reshape(steps, nb, v.shape[-1])
    y_s, tail_g, tail_v = _out_ffn(steps3(xs), steps3(attn_s), steps3(conv_s), prev_s, ffn_wts, block=(steps, sb),
                                   shift=sb, tiles_per_seq=1)
    tail_s = jnp.concatenate([tail_g, tail_v], axis=-1)
    y_sample = jnp.transpose(y_s, (1, 0, 2))
    new_kv_sample = jnp.transpose(kv_s.reshape(steps, nb, D_QK), (1, 0, 2))[None]
    new_conv_sample = jnp.transpose(ext[steps:], (1, 0, 2))[None]
    tail_s = tail_s.reshape(n_tiles, FFN_CONV_W - 1, sb, 2 * D_FF)
    new_ffn_sample = jnp.transpose(tail_s, (0, 2, 1, 3)).reshape(1, nb, FFN_CONV_W - 1, 2 * D_FF)
    return (y_prompt, y_sample, new_kv_prompt, new_conv_prompt, new_ffn_prompt,
            new_kv_sample, new_conv_sample, new_ffn_sample)
```

```python
import functools

import jax
import jax.numpy as jnp
from jax import lax
from jax.experimental import pallas as pl
from jax.experimental.pallas import tpu as pltpu

N_HEADS = 8
D_NOPE = 64
D_ROPE = 32
D_V = 64
D_C = 256
D_CQ = 384
D_ATT = N_HEADS * D_V
D_CONV = 512
CONV_W = 31
D_FF = 2816
FFN_CONV_W = 3
ROPE_THETA = 10000.0
PAGE_SIZE = 128
EPS = 1e-6
ATTN_SCALE = (D_NOPE + D_ROPE) ** -0.5
EXP2_SCALE = ATTN_SCALE * 1.4426950408889634

LANE = 128
SUBLANES = 8
D_QK = D_C + D_ROPE
D_QK_PAD = D_C + LANE
Q_TILE = 128
HEAD_GROUP = 256 // D_NOPE
FFN_CHUNKS = 2
FFN_CHUNK = D_FF // FFN_CHUNKS
PAGES_PER_STEP = 32
SAMPLE_FFN_SEQS = 32
VMEM_LIMIT = 56 * 1024 * 1024

_BF16 = jnp.bfloat16
_F32 = jnp.float32


def _rms(x, g):
    return x * lax.rsqrt(jnp.mean(x * x, axis=-1, keepdims=True) + EPS) * g


def _dot(a, b):
    return jnp.dot(a, b, preferred_element_type=_F32)


def _dot_nt(a, b):
    return lax.dot_general(a, b, (((1,), (1,)), ((), ())), preferred_element_type=_F32)


def _const_spec(shape):
    nd = len(shape)
    return pl.BlockSpec(shape, lambda *_: (0,) * nd, pipeline_mode=pl.Buffered(1))


def _inproj_kernel(x_ref, cos_ref, sin_ref, g_in_ref, w_in_ref, g_q_ref, w_qn_ref, w_qr_ref,
                   w_qs_ref, g_kv_ref, w_uk_ref, q_ref, kv_ref, kvb_ref, glu_ref, *, tq):
    tm = x_ref.shape[0]
    cos = cos_ref[...]
    sin = sin_ref[...]
    h = _rms(x_ref[...], g_in_ref[...]).astype(_BF16)
    proj = _dot(h, w_in_ref[...])
    o_kv, o_kr, o_ks, o_ga, o_gb = D_CQ, D_CQ + D_C, D_CQ + D_C + LANE, D_CQ + D_C + 2 * LANE, D_CQ + D_C + 2 * LANE + D_CONV
    cq = proj[:, :o_kv]
    ckv = proj[:, o_kv:o_kr]
    glu_ref[...] = proj[:, o_ga:o_gb] * jax.nn.sigmoid(proj[:, o_gb:])
    ckv_n = _rms(ckv, g_kv_ref[...])
    kr = proj[:, o_kr:o_ks] * cos + proj[:, o_ks:o_ga] * sin
    kv_ref[:, :D_C] = ckv_n
    kv_ref[:, D_C:] = kr[:, :D_ROPE]
    kvb_ref[:, :D_C] = ckv_n.astype(_BF16)
    kvb_ref[:, D_C:] = kr.astype(_BF16)
    cqn = _rms(cq, g_q_ref[...]).astype(_BF16)
    qn = _dot(cqn, w_qn_ref[...]).astype(_BF16)
    qr = _dot(cqn, w_qr_ref[...])
    qs = _dot(cqn, w_qs_ref[...])
    for hd in range(N_HEADS):
        grp = hd // HEAD_GROUP
        ql = _dot(qn[:, grp * 256:(grp + 1) * 256], w_uk_ref[hd]).astype(_BF16)
        qrh = (qr[:, hd * LANE:(hd + 1) * LANE] * cos + qs[:, hd * LANE:(hd + 1) * LANE] * sin).astype(_BF16)
        for s in range(tm // tq):
            q_ref[s, hd, :, :D_C] = ql[s * tq:(s + 1) * tq]
            q_ref[s, hd, :, D_C:] = qrh[s * tq:(s + 1) * tq]


def _inproj(x, cos_tab, sin_tab, wts, *, tm, tq):
    rows = x.shape[0]
    n_tab = cos_tab.shape[0] // tm
    g_in, w_in, g_q, w_qn, w_qr, w_qs, g_kv, w_uk = wts
    row_spec = lambda w: pl.BlockSpec((tm, w), lambda i: (i, 0))
    tab_spec = pl.BlockSpec((tm, LANE), lambda i: (i % n_tab, 0))
    return pl.pallas_call(
        functools.partial(_inproj_kernel, tq=tq),
        grid=(rows // tm,),
        in_specs=[row_spec(x.shape[1]), tab_spec, tab_spec] + [_const_spec(w.shape) for w in wts],
        out_specs=[pl.BlockSpec((tm // tq, N_HEADS, tq, D_QK_PAD), lambda i: (i, 0, 0, 0)),
                   row_spec(D_QK), row_spec(D_QK_PAD), row_spec(D_CONV)],
        out_shape=[jax.ShapeDtypeStruct((rows // tq, N_HEADS, tq, D_QK_PAD), _BF16),
                   jax.ShapeDtypeStruct((rows, D_QK), _F32),
                   jax.ShapeDtypeStruct((rows, D_QK_PAD), _BF16),
                   jax.ShapeDtypeStruct((rows, D_CONV), _F32)],
        compiler_params=pltpu.CompilerParams(dimension_semantics=("parallel",), vmem_limit_bytes=VMEM_LIMIT),
        name="inproj",
    )(x, cos_tab, sin_tab, *wts)


CONV_HALO = 32
CONV_ROWS = 16
CONV_UNROLL = 8


def _conv_post(y, g_ln, b_ln, g_out):
    mu = jnp.mean(y, axis=-1, keepdims=True)
    yc = y - mu
    z = yc * lax.rsqrt(jnp.mean(yc * yc, axis=-1, keepdims=True) + EPS) * g_ln + b_ln
    return _rms(z * jax.nn.sigmoid(z), g_out).astype(_BF16)


def _conv_prompt_kernel(cur_ref, halo_ref, w_ref, b_ref, g_ln_ref, b_ln_ref, g_out_ref, o_ref, ext_ref, sh_ref):
    tc = cur_ref.shape[1]
    first = pl.program_id(1) == 0
    ext_ref[:CONV_HALO] = jnp.where(first, 0.0, halo_ref[0])
    ext_ref[CONV_HALO:CONV_HALO + tc] = cur_ref[0]
    ext_ref[CONV_HALO + tc:] = jnp.zeros((SUBLANES, D_CONV), _F32)
    off = CONV_HALO - (CONV_W - 1)
    for r in range(SUBLANES):
        sh_ref[r] = ext_ref[r:r + sh_ref.shape[1], :]

    def body(it, carry):
        for u in range(CONV_UNROLL):
            base = pl.multiple_of((it * CONV_UNROLL + u) * CONV_ROWS, CONV_ROWS)
            acc = jnp.broadcast_to(b_ref[...], (CONV_ROWS, D_CONV))
            for k in range(CONV_W):
                r = (off + k) % SUBLANES
                acc = acc + w_ref[k:k + 1, :] * sh_ref[r, pl.ds(base + (off + k - r), CONV_ROWS), :]
            o_ref[pl.ds(base, CONV_ROWS), :] = _conv_post(acc, g_ln_ref[...], b_ln_ref[...], g_out_ref[...])
        return carry

    lax.fori_loop(0, tc // (CONV_ROWS * CONV_UNROLL), body, 0)


def _conv_prompt(glu, wts, *, batch, seq, tc):
    glu3 = glu.reshape(batch, seq, D_CONV)
    hb = tc // CONV_HALO
    return pl.pallas_call(
        _conv_prompt_kernel,
        grid=(batch, seq // tc),
        in_specs=[pl.BlockSpec((1, tc, D_CONV), lambda b, i: (b, i, 0)),
                  pl.BlockSpec((1, CONV_HALO, D_CONV), lambda b, i: (b, jnp.maximum(i * hb - 1, 0), 0))]
                 + [_const_spec(w.shape) for w in wts],
        out_specs=pl.BlockSpec((tc, D_CONV), lambda b, i: (b * (seq // tc) + i, 0)),
        out_shape=jax.ShapeDtypeStruct((batch * seq, D_CONV), _BF16),
        scratch_shapes=[pltpu.VMEM((CONV_HALO + tc + SUBLANES, D_CONV), _F32),
                        pltpu.VMEM((SUBLANES, CONV_HALO + tc, D_CONV), _F32)],
        compiler_params=pltpu.CompilerParams(dimension_semantics=("parallel", "parallel"), vmem_limit_bytes=VMEM_LIMIT),
        name="conv_prompt",
    )(glu3, glu3, *wts)


def _conv_sample_kernel(ext_ref, w_ref, b_ref, g_ln_ref, b_ln_ref, g_out_ref, o_ref):
    steps, nb = o_ref.shape[0], o_ref.shape[1]
    chunks = nb // CONV_ROWS

    def body(it, carry):
        t = it // chunks
        base = pl.multiple_of((it % chunks) * CONV_ROWS, CONV_ROWS)
        acc = jnp.broadcast_to(b_ref[...], (CONV_ROWS, D_CONV))
        for k in range(CONV_W):
            acc = acc + w_ref[k:k + 1, :] * ext_ref[t + k, pl.ds(base, CONV_ROWS), :]
        o_ref[t, pl.ds(base, CONV_ROWS), :] = _conv_post(acc, g_ln_ref[...], b_ln_ref[...], g_out_ref[...])
        return carry

    lax.fori_loop(0, steps * chunks, body, 0)


def _conv_sample(ext, wts, *, steps):
    nb = ext.shape[1]
    return pl.pallas_call(
        _conv_sample_kernel,
        out_shape=jax.ShapeDtypeStruct((steps, nb, D_CONV), _BF16),
        compiler_params=pltpu.CompilerParams(vmem_limit_bytes=VMEM_LIMIT),
        name="conv_sample",
    )(ext, *wts)


def _softmax_step(s, m_prev, l_prev, acc_prev, v):
    m_new = jnp.maximum(m_prev, jnp.max(s, axis=-1, keepdims=True))
    alpha = jnp.exp(m_prev - m_new)
    p = jnp.exp(s - m_new)
    l_new = alpha * l_prev + jnp.sum(p, axis=-1, keepdims=True)
    acc_new = alpha * acc_prev + _dot(p.astype(_BF16), v)
    return m_new, l_new, acc_new


def _attn_out(o, rows_per_head, w_uv_ref, g_ref):
    parts = []
    for pr in range(N_HEADS // 2):
        acc = None
        for hd in (2 * pr, 2 * pr + 1):
            oh = o[hd * rows_per_head:(hd + 1) * rows_per_head].astype(_BF16)
            term = _dot(oh, w_uv_ref[hd])
            acc = term if acc is None else acc + term
        parts.append(acc)
    attn = jnp.concatenate(parts, axis=-1)
    return _rms(attn, g_ref[...]).astype(_BF16)


def _attn_prompt_kernel(q_ref, kv_ref, w_uv_ref, g_ref, o_ref, s_ref, s_next_ref, m_ref, l_ref, acc_ref, *, tk):
    tq = q_ref.shape[2]
    rows = N_HEADS * tq
    qi = pl.program_id(1)
    m_ref[...] = jnp.full(m_ref.shape, -jnp.inf, _F32)
    l_ref[...] = jnp.zeros(l_ref.shape, _F32)
    acc_ref[...] = jnp.zeros(acc_ref.shape, _F32)
    n_full = (qi * tq) // tk

    def keys(c):
        return kv_ref[0, pl.ds(pl.multiple_of(c * tk, tk), tk), :]

    def scores(c):
        return _dot_nt(q_ref[0].reshape(rows, D_QK_PAD), keys(c))

    def softmax_pv(s, v):
        width = s.shape[1]
        m_prev = m_ref[...]
        m_new = jnp.maximum(m_prev, jnp.max(s, axis=-1, keepdims=True))
        p = jnp.exp2((s - jnp.tile(m_new, (1, width // LANE))) * EXP2_SCALE)
        alpha = jnp.exp2((m_prev - m_new) * EXP2_SCALE)
        m_ref[...] = m_new
        l_ref[...] = alpha * l_ref[...] + jnp.sum(p, axis=-1, keepdims=True)
        acc_ref[...] = jnp.tile(alpha, (1, D_C // LANE)) * acc_ref[...] + _dot(p.astype(_BF16), v)

    s_ref[...] = scores(0)

    def body(c, carry):
        s_next_ref[...] = scores(c + 1)
        softmax_pv(s_ref[...], keys(c)[:, :D_C])
        s_ref[...] = s_next_ref[...]
        return carry

    lax.fori_loop(0, n_full, body, 0)
    for r in range(tk // tq):
        @pl.when(qi % (tk // tq) == r)
        def _():
            width = (r + 1) * tq
            s = s_ref[:, :width]
            t_idx = r * tq + lax.broadcasted_iota(jnp.int32, (rows, width), 0) % tq
            k_idx = lax.broadcasted_iota(jnp.int32, (rows, width), 1)
            s = jnp.where(k_idx <= t_idx, s, -jnp.inf)
            softmax_pv(s, keys(n_full)[:width, :D_C])

    o_ref[...] = _attn_out(acc_ref[...] / jnp.tile(l_ref[...], (1, D_C // LANE)), tq, w_uv_ref, g_ref)


def _attn_prompt(q, kvb, w_uv, g, *, batch, seq, tk):
    tq = q.shape[2]
    nq = seq // tq
    rows = N_HEADS * tq
    return pl.pallas_call(
        functools.partial(_attn_prompt_kernel, tk=tk),
        grid=(batch, nq),
        in_specs=[pl.BlockSpec((1, N_HEADS, tq, D_QK_PAD), lambda b, i: (b * nq + i, 0, 0, 0)),
                  pl.BlockSpec((1, seq, D_QK_PAD), lambda b, i: (b, 0, 0)),
                  _const_spec(w_uv.shape), _const_spec(g.shape)],
        out_specs=pl.BlockSpec((tq, D_ATT), lambda b, i: (b * nq + i, 0)),
        out_shape=jax.ShapeDtypeStruct((batch * seq, D_ATT), _BF16),
        scratch_shapes=[pltpu.VMEM((rows, tk), _F32), pltpu.VMEM((rows, tk), _F32),
                        pltpu.VMEM((rows, LANE), _F32), pltpu.VMEM((rows, LANE), _F32),
                        pltpu.VMEM((rows, D_C), _F32)],
        compiler_params=pltpu.CompilerParams(dimension_semantics=("parallel", "parallel"), vmem_limit_bytes=VMEM_LIMIT),
        name="attn_prompt",
    )(q, kvb.reshape(batch, seq, D_QK_PAD), w_uv, g)


def _attn_sample_kernel(pt_ref, q_ref, kvn_ref, w_uv_ref, g_ref, *rest, steps):
    page_refs = rest[:PAGES_PER_STEP]
    o_ref, kt_ref, kn_ref, m_ref, l_ref, acc_ref = rest[PAGES_PER_STEP:]
    c = pl.program_id(1)
    rows = q_ref.shape[1]

    @pl.when(c == 0)
    def _():
        kt_ref[D_QK:, :] = jnp.zeros((D_QK_PAD - D_QK, kt_ref.shape[1]), _BF16)
        m_ref[...] = jnp.full(m_ref.shape, -jnp.inf, _F32)
        l_ref[...] = jnp.zeros(l_ref.shape, _F32)
        acc_ref[...] = jnp.zeros(acc_ref.shape, _F32)

    for p, page_ref in enumerate(page_refs):
        kt_ref[:D_QK, p * PAGE_SIZE:(p + 1) * PAGE_SIZE] = page_ref[0].astype(_BF16)
    s = _dot(q_ref[0], kt_ref[...]) * ATTN_SCALE
    m_prev = m_ref[...]
    m_new = jnp.maximum(m_prev, jnp.max(s, axis=-1, keepdims=True))
    alpha = jnp.exp(m_prev - m_new)
    p_un = jnp.exp(s - m_new)
    m_ref[...] = m_new
    l_ref[...] = alpha * l_ref[...] + jnp.sum(p_un, axis=-1, keepdims=True)
    acc_ref[...] = alpha * acc_ref[...] + _dot_nt(p_un.astype(_BF16), kt_ref[:D_C, :])

    @pl.when(c == pl.num_programs(1) - 1)
    def _():
        kn_ref[...] = jnp.zeros(kn_ref.shape, _BF16)
        kn_ref[:steps, :] = kvn_ref[0]
        kn = kn_ref[...]
        sn = _dot_nt(q_ref[0], kn) * ATTN_SCALE
        t_pos = lax.broadcasted_iota(jnp.int32, (rows, PAGE_SIZE), 0) % steps
        j_pos = lax.broadcasted_iota(jnp.int32, (rows, PAGE_SIZE), 1)
        sn = jnp.where(j_pos <= t_pos, sn, -jnp.inf)
        _, l2, acc2 = _softmax_step(sn, m_ref[...], l_ref[...], acc_ref[...], kn[:, :D_C])
        o_ref[0] = _attn_out(acc2 / l2, steps, w_uv_ref, g_ref)


def _attn_sample(page_table, q, kv_new, cache_t, w_uv, g):
    nb, rows, _ = q.shape
    steps = kv_new.shape[1]
    n_pages = page_table.shape[1]
    n_chunks = n_pages // PAGES_PER_STEP

    def page_spec(p):
        return pl.BlockSpec((1, D_QK, PAGE_SIZE),
                            lambda b, c, pt: (pt[b * n_pages + c * PAGES_PER_STEP + p], 0, 0))

    const = lambda shape: pl.BlockSpec(shape, lambda b, c, pt: (0,) * len(shape), pipeline_mode=pl.Buffered(1))
    return pl.pallas_call(
        functools.partial(_attn_sample_kernel, steps=steps),
        grid_spec=pltpu.PrefetchScalarGridSpec(
            num_scalar_prefetch=1,
            grid=(nb, n_chunks),
            in_specs=[pl.BlockSpec((1, rows, D_QK_PAD), lambda b, c, pt: (b, 0, 0)),
                      pl.BlockSpec((1, steps, D_QK_PAD), lambda b, c, pt: (b, 0, 0)),
                      const(w_uv.shape), const(g.shape)]
                     + [page_spec(p) for p in range(PAGES_PER_STEP)],
            out_specs=pl.BlockSpec((1, steps, D_ATT), lambda b, c, pt: (b, 0, 0)),
            scratch_shapes=[pltpu.VMEM((D_QK_PAD, PAGES_PER_STEP * PAGE_SIZE), _BF16),
                            pltpu.VMEM((PAGE_SIZE, D_QK_PAD), _BF16),
                            pltpu.VMEM((rows, 1), _F32), pltpu.VMEM((rows, 1), _F32),
                            pltpu.VMEM((rows, D_C), _F32)]),
        out_shape=jax.ShapeDtypeStruct((nb, steps, D_ATT), _BF16),
        compiler_params=pltpu.CompilerParams(dimension_semantics=("parallel", "arbitrary"), vmem_limit_bytes=VMEM_LIMIT),
        name="attn_sample",
    )(page_table.reshape(-1), q, kv_new, w_uv, g, *([cache_t] * PAGES_PER_STEP))


def _out_ffn_kernel(x_ref, attn_ref, conv_ref, prev_g_ref, prev_v_ref, w_oa_ref, w_oc_ref, g_ffn_ref,
                    w_up_g_ref, w_up_v_ref, w_dw_g_ref, w_dw_v_ref, b_dw_g_ref, b_dw_v_ref, w_down_ref, g_fin_ref,
                    y_ref, tail_g_ref, tail_v_ref, ubuf_ref, carry_ref, acc_ref, h2_ref, *, shift, tiles_per_seq):
    gb, sb, d_model = x_ref.shape
    tm = gb * sb
    hist = prev_g_ref.shape[1]
    j = pl.program_id(1)
    n_chunks = pl.num_programs(1)
    first = pl.program_id(0) % tiles_per_seq == 0

    @pl.when(j == 0)
    def _():
        flat = lambda ref: ref[...].reshape(tm, ref.shape[2])
        acc_ref[...] = flat(x_ref) + _dot(flat(attn_ref), w_oa_ref[...]) + _dot(flat(conv_ref), w_oc_ref[...])
        h2_ref[...] = _rms(acc_ref[...], g_ffn_ref[...]).astype(_BF16)

    halves = []
    for part, (prev_ref, w_up_ref, w_dw_ref, b_dw_ref, tail_ref) in enumerate((
            (prev_g_ref, w_up_g_ref, w_dw_g_ref, b_dw_g_ref, tail_g_ref),
            (prev_v_ref, w_up_v_ref, w_dw_v_ref, b_dw_v_ref, tail_v_ref))):
        slot = part * n_chunks + j

        @pl.when(first)
        def _():
            ubuf_ref[part, :hist] = prev_ref[0]

        @pl.when(jnp.logical_not(first))
        def _():
            ubuf_ref[part, :hist] = carry_ref[slot]

        ubuf_ref[part, hist:] = _dot(h2_ref[...], w_up_ref[...])
        last_rows = ubuf_ref[part, tm:]
        carry_ref[slot] = last_rows
        tail_ref[0] = last_rows
        halves.append(b_dw_ref[...]
                      + w_dw_ref[0:1] * ubuf_ref[part, hist - 2 * shift:hist - 2 * shift + tm]
                      + w_dw_ref[1:2] * ubuf_ref[part, hist - shift:hist - shift + tm]
                      + w_dw_ref[2:3] * ubuf_ref[part, hist:])
    act = (halves[0] * jax.nn.sigmoid(halves[0]) * halves[1]).astype(_BF16)
    acc_ref[...] += _dot(act, w_down_ref[...])

    @pl.when(j == n_chunks - 1)
    def _():
        y_ref[...] = _rms(acc_ref[...], g_fin_ref[...]).reshape(gb, sb, d_model)


def _out_ffn(x, attn_n, conv_n, prev, wts, *, block, shift, tiles_per_seq):
    g_dim, s_dim, d_model = x.shape
    gb, sb = block
    n_s = s_dim // sb
    n_seq, hist, _ = prev.shape
    w_oa, w_oc, g_ffn, w_up, w_dw, b_dw, w_down, g_fin = wts
    fc, nj = FFN_CHUNK, FFN_CHUNKS
    row_spec = lambda w: pl.BlockSpec((gb, sb, w), lambda i, j: (i // n_s, i % n_s, 0))
    const = lambda w: pl.BlockSpec(w.shape, lambda i, j: (0,) * w.ndim, pipeline_mode=pl.Buffered(1))
    hist_spec = lambda part: pl.BlockSpec((1, hist, fc), lambda i, j: (i // tiles_per_seq, 0, part * nj + j))
    cols_spec = lambda rows, part: pl.BlockSpec((rows, fc), lambda i, j: (0, part * nj + j))
    tail_spec = pl.BlockSpec((1, hist, fc), lambda i, j: (i, 0, j))
    n_tiles = (g_dim // gb) * n_s
    return pl.pallas_call(
        functools.partial(_out_ffn_kernel, shift=shift, tiles_per_seq=tiles_per_seq),
        grid=(n_tiles, nj),
        in_specs=[row_spec(d_model), row_spec(D_ATT), row_spec(D_CONV), hist_spec(0), hist_spec(1),
                  const(w_oa), const(w_oc), const(g_ffn),
                  cols_spec(d_model, 0), cols_spec(d_model, 1), cols_spec(SUBLANES, 0), cols_spec(SUBLANES, 1),
                  cols_spec(1, 0), cols_spec(1, 1), pl.BlockSpec((fc, d_model), lambda i, j: (j, 0)), const(g_fin)],
        out_specs=[row_spec(d_model), tail_spec, tail_spec],
        out_shape=[jax.ShapeDtypeStruct(x.shape, _F32)] + [jax.ShapeDtypeStruct((n_tiles, hist, D_FF), _F32)] * 2,
        scratch_shapes=[pltpu.VMEM((2, hist + gb * sb, fc), _F32), pltpu.VMEM((2 * nj, hist, fc), _F32),
                        pltpu.VMEM((gb * sb, d_model), _F32), pltpu.VMEM((gb * sb, d_model), _BF16)],
        compiler_params=pltpu.CompilerParams(dimension_semantics=("arbitrary", "arbitrary"),
                                             vmem_limit_bytes=VMEM_LIMIT),
        name="out_ffn",
    )(x, attn_n, conv_n, prev, prev, w_oa, w_oc, g_ffn, w_up, w_up, w_dw, w_dw, b_dw, b_dw, w_down, g_fin)


def _rope_tables(pos):
    half = D_ROPE // 2
    inv = ROPE_THETA ** (-(jnp.arange(half, dtype=_F32) * 2.0 / D_ROPE))
    ang = pos.astype(_F32)[:, None] * inv[None, :]
    cos, sin = jnp.cos(ang), jnp.sin(ang)
    pad = ((0, 0), (0, LANE - D_ROPE))
    return jnp.pad(jnp.concatenate([cos, cos], axis=-1), pad), jnp.pad(jnp.concatenate([-sin, sin], axis=-1), pad)


def _swap_halves(w):
    half = D_ROPE // 2
    return jnp.concatenate([w[..., half:], w[..., :half]], axis=-1)


def _pad_lanes(w):
    return jnp.pad(w, [(0, 0)] * (w.ndim - 1) + [(0, LANE - w.shape[-1])])


def kernel(x_prompt, x_sample, cache_kv_latent, state_conv, state_ffn_conv, page_table, g_attn_norm, w_in, g_q_norm, w_uq, g_kv_norm, w_uk, w_uv, w_dw, b_dw, g_conv_ln, b_conv_ln, g_attn_out, g_conv_out, w_o, g_ffn_norm, w_up, w_ffn_dw, b_ffn_dw, w_down, g_final):
    batch, seq, d_model = x_prompt.shape
    nb, steps, _ = x_sample.shape
    n_pages = page_table.shape[1]
    past_len = n_pages * PAGE_SIZE
    assert w_in.shape[0] == 1, "single-layer trunk"
    assert seq % Q_TILE == 0 and nb % CONV_ROWS == 0 and n_pages % PAGES_PER_STEP == 0 and steps <= PAGE_SIZE

    row = lambda v: v.reshape(1, -1)
    o1, o2, o3 = D_CQ, D_CQ + D_C, D_CQ + D_C + D_ROPE
    wi = w_in[0]
    w_in_ext = jnp.concatenate([wi[:, :o2], _pad_lanes(wi[:, o2:o3]), _pad_lanes(_swap_halves(wi[:, o2:o3])),
                                wi[:, o3:]], axis=1).astype(_BF16)
    wq = w_uq[0]
    w_qn = wq[:, :, :D_NOPE].reshape(D_CQ, N_HEADS * D_NOPE).astype(_BF16)
    w_qr = _pad_lanes(wq[:, :, D_NOPE:]).reshape(D_CQ, N_HEADS * LANE).astype(_BF16)
    w_qs = _pad_lanes(_swap_halves(wq[:, :, D_NOPE:])).reshape(D_CQ, N_HEADS * LANE).astype(_BF16)
    wk = jnp.transpose(w_uk[0], (1, 2, 0))
    sel = (jnp.arange(HEAD_GROUP)[None, :] == (jnp.arange(N_HEADS) % HEAD_GROUP)[:, None]).astype(_F32)
    w_uk_z = (sel[:, :, None, None] * wk[:, None]).reshape(N_HEADS, HEAD_GROUP * D_NOPE, D_C).astype(_BF16)
    wv = jnp.transpose(w_uv[0], (1, 0, 2))
    sel2 = (jnp.arange(2)[None, :] == (jnp.arange(N_HEADS) % 2)[:, None]).astype(_F32)
    w_uv_z = (wv[:, :, None, :] * sel2[:, None, :, None]).reshape(N_HEADS, D_C, 2 * D_V).astype(_BF16)
    in_wts = (row(g_attn_norm[0]), w_in_ext, row(g_q_norm[0]), w_qn, w_qr, w_qs, row(g_kv_norm[0]), w_uk_z)
    conv_wts = (jnp.pad(w_dw[0], ((0, 1), (0, 0))), row(b_dw[0]), row(g_conv_ln[0]), row(b_conv_ln[0]),
                row(g_conv_out[0]))
    g_ao = row(g_attn_out[0])
    wo = w_o[0].astype(_BF16)
    w_fdw = jnp.pad(w_ffn_dw[0], ((0, SUBLANES - FFN_CONV_W), (0, 0)))
    ffn_wts = (wo[:D_ATT], wo[D_ATT:], row(g_ffn_norm[0]), w_up[0].astype(_BF16), w_fdw, row(b_ffn_dw[0]),
               w_down[0].astype(_BF16), row(g_final))

    rows_p = batch * seq
    tm = min(512, seq)
    xp = x_prompt.reshape(rows_p, d_model)
    cos_p, sin_p = _rope_tables(jnp.arange(seq, dtype=jnp.int32))
    q_p, kv_p, kvb_p, glu_p = _inproj(xp, cos_p, sin_p, in_wts, tm=tm, tq=Q_TILE)
    conv_p = _conv_prompt(glu_p, conv_wts, batch=batch, seq=seq, tc=tm)
    attn_p = _attn_prompt(q_p, kvb_p, w_uv_z, g_ao, batch=batch, seq=seq, tk=min(512, seq))
    hist_p = SUBLANES
    prev_p = jnp.zeros((batch, hist_p, 2 * D_FF), _F32)
    tiled = lambda v: v.reshape(rows_p // tm, tm, v.shape[-1])
    y_p, tail_g, tail_v = _out_ffn(tiled(xp), tiled(attn_p), tiled(conv_p), prev_p, ffn_wts, block=(1, tm), shift=1,
                                   tiles_per_seq=seq // tm)
    last = seq // tm - 1
    tail_p = jnp.concatenate([tail_g[last::seq // tm], tail_v[last::seq // tm]], axis=-1)
    y_prompt = y_p.reshape(batch, seq, d_model)
    new_kv_prompt = kv_p.reshape(1, batch, seq, D_QK)
    new_conv_prompt = glu_p.reshape(batch, seq, D_CONV)[None, :, seq - (CONV_W - 1):]
    new_ffn_prompt = tail_p[None, :, hist_p - (FFN_CONV_W - 1):]

    rows_s = nb * steps
    xs = jnp.transpose(x_sample, (1, 0, 2)).reshape(rows_s, d_model)
    pos_s = past_len + jnp.arange(rows_s, dtype=jnp.int32) // nb
    cos_s, sin_s = _rope_tables(pos_s)
    tq_s = min(Q_TILE, nb)
    q_s, kv_s, kvb_s, glu_s = _inproj(xs, cos_s, sin_s, in_wts, tm=rows_s, tq=tq_s)
    q_s = q_s.reshape(steps, nb // tq_s, N_HEADS, tq_s, D_QK_PAD)
    q_s = jnp.transpose(q_s, (1, 3, 2, 0, 4)).reshape(nb, N_HEADS * steps, D_QK_PAD)
    kvn_s = jnp.transpose(kvb_s.reshape(steps, nb, D_QK_PAD), (1, 0, 2))
    ext = jnp.concatenate([jnp.transpose(state_conv[0], (1, 0, 2)), glu_s.reshape(steps, nb, D_CONV)], axis=0)
    conv_s = _conv_sample(ext, conv_wts, steps=steps).reshape(rows_s, D_CONV)
    attn_s = _attn_sample(page_table, q_s, kvn_s, jnp.swapaxes(cache_kv_latent[0], 1, 2), w_uv_z, g_ao)
    attn_s = jnp.transpose(attn_s, (1, 0, 2)).reshape(rows_s, D_ATT)
    sb = min(SAMPLE_FFN_SEQS, nb)
    n_tiles = nb // sb
    hist_s = (FFN_CONV_W - 1) * sb
    prev_s = jnp.transpose(state_ffn_conv[0].reshape(n_tiles, sb, FFN_CONV_W - 1, 2 * D_FF), (0, 2, 1, 3))
    prev_s = prev_s.reshape(n_tiles, hist_s, 2 * D_FF)
    steps3 = lambda v: v.reshape(steps, nb, v.shape[-1])
    y_s, tail_g, tail_v = _out_ffn(steps3(xs), steps3(attn_s), steps3(conv_s), prev_s, ffn_wts, block=(steps, sb),
                                   shift=sb, tiles_per_seq=1)
    tail_s = jnp.concatenate([tail_g, tail_v], axis=-1)
    y_sample = jnp.transpose(y_s, (1, 0, 2))
    new_kv_sample = jnp.transpose(kv_s.reshape(steps, nb, D_QK), (1, 0, 2))[None]
    new_conv_sample = jnp.transpose(ext[steps:], (1, 0, 2))[None]
    tail_s = tail_s.reshape(n_tiles, FFN_CONV_W - 1, sb, 2 * D_FF)
    new_ffn_sample = jnp.transpose(tail_s, (0, 2, 1, 3)).reshape(1, nb, FFN_CONV_W - 1, 2 * D_FF)
    return (y_prompt, y_sample, new_kv_prompt, new_conv_prompt, new_ffn_prompt,
            new_kv_sample, new_conv_sample, new_ffn_sample)
```

```python
import functools

import jax
import jax.numpy as jnp
from jax import lax
from jax.experimental import pallas as pl
from jax.experimental.pallas import tpu as pltpu

N_HEADS = 8
D_NOPE = 64
D_ROPE = 32
D_V = 64
D_C = 256
D_CQ = 384
D_ATT = N_HEADS * D_V
D_CONV = 512
CONV_W = 31
D_FF = 2816
FFN_CONV_W = 3
ROPE_THETA = 10000.0
PAGE_SIZE = 128
EPS = 1e-6
ATTN_SCALE = (D_NOPE + D_ROPE) ** -0.5
EXP2_SCALE = ATTN_SCALE * 1.4426950408889634

LANE = 128
SUBLANES = 8
D_QK = D_C + D_ROPE
D_QK_PAD = D_C + LANE
Q_TILE = 128
HEAD_GROUP = 256 // D_NOPE
FFN_CHUNKS = 1
FFN_CHUNK = D_FF // FFN_CHUNKS
PAGES_PER_STEP = 32
SAMPLE_FFN_SEQS = 32
VMEM_LIMIT = 56 * 1024 * 1024

_BF16 = jnp.bfloat16
_F32 = jnp.float32


def _rms(x, g):
    return x * lax.rsqrt(jnp.mean(x * x, axis=-1, keepdims=True) + EPS) * g


def _dot(a, b):
    return jnp.dot(a, b, preferred_element_type=_F32)


def _dot_nt(a, b):
    return lax.dot_general(a, b, (((1,), (1,)), ((), ())), preferred_element_type=_F32)


def _const_spec(shape):
    nd = len(shape)
    return pl.BlockSpec(shape, lambda *_: (0,) * nd, pipeline_mode=pl.Buffered(1))


def _inproj_kernel(x_ref, cos_ref, sin_ref, g_in_ref, w_in_ref, g_q_ref, w_qn_ref, w_qr_ref,
                   w_qs_ref, g_kv_ref, w_uk_ref, q_ref, kv_ref, kvb_ref, glu_ref, *, tq):
    tm = x_ref.shape[0]
    cos = cos_ref[...]
    sin = sin_ref[...]
    h = _rms(x_ref[...], g_in_ref[...]).astype(_BF16)
    proj = _dot(h, w_in_ref[...])
    o_kv, o_kr, o_ks, o_ga, o_gb = D_CQ, D_CQ + D_C, D_CQ + D_C + LANE, D_CQ + D_C + 2 * LANE, D_CQ + D_C + 2 * LANE + D_CONV
    cq = proj[:, :o_kv]
    ckv = proj[:, o_kv:o_kr]
    glu_ref[...] = proj[:, o_ga:o_gb] * jax.nn.sigmoid(proj[:, o_gb:])
    ckv_n = _rms(ckv, g_kv_ref[...])
    kr = proj[:, o_kr:o_ks] * cos + proj[:, o_ks:o_ga] * sin
    kv_ref[:, :D_C] = ckv_n
    kv_ref[:, D_C:] = kr[:, :D_ROPE]
    kvb_ref[:, :D_C] = ckv_n.astype(_BF16)
    kvb_ref[:, D_C:] = kr.astype(_BF16)
    cqn = _rms(cq, g_q_ref[...]).astype(_BF16)
    qn = _dot(cqn, w_qn_ref[...]).astype(_BF16)
    qr = _dot(cqn, w_qr_ref[...])
    qs = _dot(cqn, w_qs_ref[...])
    for hd in range(N_HEADS):
        grp = hd // HEAD_GROUP
        ql = _dot(qn[:, grp * 256:(grp + 1) * 256], w_uk_ref[hd]).astype(_BF16)
        qrh = (qr[:, hd * LANE:(hd + 1) * LANE] * cos + qs[:, hd * LANE:(hd + 1) * LANE] * sin).astype(_BF16)
        for s in range(tm // tq):
            q_ref[s, hd, :, :D_C] = ql[s * tq:(s + 1) * tq]
            q_ref[s, hd, :, D_C:] = qrh[s * tq:(s + 1) * tq]


def _inproj(x, cos_tab, sin_tab, wts, *, tm, tq):
    rows = x.shape[0]
    n_tab = cos_tab.shape[0] // tm
    g_in, w_in, g_q, w_qn, w_qr, w_qs, g_kv, w_uk = wts
    row_spec = lambda w: pl.BlockSpec((tm, w), lambda i: (i, 0))
    tab_spec = pl.BlockSpec((tm, LANE), lambda i: (i % n_tab, 0))
    return pl.pallas_call(
        functools.partial(_inproj_kernel, tq=tq),
        grid=(rows // tm,),
        in_specs=[row_spec(x.shape[1]), tab_spec, tab_spec] + [_const_spec(w.shape) for w in wts],
        out_specs=[pl.BlockSpec((tm // tq, N_HEADS, tq, D_QK_PAD), lambda i: (i, 0, 0, 0)),
                   row_spec(D_QK), row_spec(D_QK_PAD), row_spec(D_CONV)],
        out_shape=[jax.ShapeDtypeStruct((rows // tq, N_HEADS, tq, D_QK_PAD), _BF16),
                   jax.ShapeDtypeStruct((rows, D_QK), _F32),
                   jax.ShapeDtypeStruct((rows, D_QK_PAD), _BF16),
                   jax.ShapeDtypeStruct((rows, D_CONV), _F32)],
        compiler_params=pltpu.CompilerParams(dimension_semantics=("parallel",), vmem_limit_bytes=VMEM_LIMIT),
        name="inproj",
    )(x, cos_tab, sin_tab, *wts)


CONV_HALO = 32
CONV_ROWS = 16
CONV_UNROLL = 8


def _conv_post(y, g_ln, b_ln, g_out):
    mu = jnp.mean(y, axis=-1, keepdims=True)
    yc = y - mu
    z = yc * lax.rsqrt(jnp.mean(yc * yc, axis=-1, keepdims=True) + EPS) * g_ln + b_ln
    return _rms(z * jax.nn.sigmoid(z), g_out).astype(_BF16)


def _conv_prompt_kernel(cur_ref, halo_ref, w_ref, b_ref, g_ln_ref, b_ln_ref, g_out_ref, o_ref, ext_ref, sh_ref):
    tc = cur_ref.shape[1]
    first = pl.program_id(1) == 0
    ext_ref[:CONV_HALO] = jnp.where(first, 0.0, halo_ref[0])
    ext_ref[CONV_HALO:CONV_HALO + tc] = cur_ref[0]
    ext_ref[CONV_HALO + tc:] = jnp.zeros((SUBLANES, D_CONV), _F32)
    off = CONV_HALO - (CONV_W - 1)
    for r in range(SUBLANES):
        sh_ref[r] = ext_ref[r:r + sh_ref.shape[1], :]

    def body(it, carry):
        for u in range(CONV_UNROLL):
            base = pl.multiple_of((it * CONV_UNROLL + u) * CONV_ROWS, CONV_ROWS)
            acc = jnp.broadcast_to(b_ref[...], (CONV_ROWS, D_CONV))
            for k in range(CONV_W):
                r = (off + k) % SUBLANES
                acc = acc + w_ref[k:k + 1, :] * sh_ref[r, pl.ds(base + (off + k - r), CONV_ROWS), :]
            o_ref[pl.ds(base, CONV_ROWS), :] = _conv_post(acc, g_ln_ref[...], b_ln_ref[...], g_out_ref[...])
        return carry

    lax.fori_loop(0, tc // (CONV_ROWS * CONV_UNROLL), body, 0)


def _conv_prompt(glu, wts, *, batch, seq, tc):
    glu3 = glu.reshape(batch, seq, D_CONV)
    hb = tc // CONV_HALO
    return pl.pallas_call(
        _conv_prompt_kernel,
        grid=(batch, seq // tc),
        in_specs=[pl.BlockSpec((1, tc, D_CONV), lambda b, i: (b, i, 0)),
                  pl.BlockSpec((1, CONV_HALO, D_CONV), lambda b, i: (b, jnp.maximum(i * hb - 1, 0), 0))]
                 + [_const_spec(w.shape) for w in wts],
        out_specs=pl.BlockSpec((tc, D_CONV), lambda b, i: (b * (seq // tc) + i, 0)),
        out_shape=jax.ShapeDtypeStruct((batch * seq, D_CONV), _BF16),
        scratch_shapes=[pltpu.VMEM((CONV_HALO + tc + SUBLANES, D_CONV), _F32),
                        pltpu.VMEM((SUBLANES, CONV_HALO + tc, D_CONV), _F32)],
        compiler_params=pltpu.CompilerParams(dimension_semantics=("parallel", "parallel"), vmem_limit_bytes=VMEM_LIMIT),
        name="conv_prompt",
    )(glu3, glu3, *wts)


def _conv_sample_kernel(ext_ref, w_ref, b_ref, g_ln_ref, b_ln_ref, g_out_ref, o_ref):
    steps, nb = o_ref.shape[0], o_ref.shape[1]
    chunks = nb // CONV_ROWS

    def body(it, carry):
        t = it // chunks
        base = pl.multiple_of((it % chunks) * CONV_ROWS, CONV_ROWS)
        acc = jnp.broadcast_to(b_ref[...], (CONV_ROWS, D_CONV))
        for k in range(CONV_W):
            acc = acc + w_ref[k:k + 1, :] * ext_ref[t + k, pl.ds(base, CONV_ROWS), :]
        o_ref[t, pl.ds(base, CONV_ROWS), :] = _conv_post(acc, g_ln_ref[...], b_ln_ref[...], g_out_ref[...])
        return carry

    lax.fori_loop(0, steps * chunks, body, 0)


def _conv_sample(ext, wts, *, steps):
    nb = ext.shape[1]
    return pl.pallas_call(
        _conv_sample_kernel,
        out_shape=jax.ShapeDtypeStruct((steps, nb, D_CONV), _BF16),
        compiler_params=pltpu.CompilerParams(vmem_limit_bytes=VMEM_LIMIT),
        name="conv_sample",
    )(ext, *wts)


def _softmax_step(s, m_prev, l_prev, acc_prev, v):
    m_new = jnp.maximum(m_prev, jnp.max(s, axis=-1, keepdims=True))
    alpha = jnp.exp(m_prev - m_new)
    p = jnp.exp(s - m_new)
    l_new = alpha * l_prev + jnp.sum(p, axis=-1, keepdims=True)
    acc_new = alpha * acc_prev + _dot(p.astype(_BF16), v)
    return m_new, l_new, acc_new


def _attn_out(o, rows_per_head, w_uv_ref, g_ref):
    parts = []
    for pr in range(N_HEADS // 2):
        acc = None
        for hd in (2 * pr, 2 * pr + 1):
            oh = o[hd * rows_per_head:(hd + 1) * rows_per_head].astype(_BF16)
            term = _dot(oh, w_uv_ref[hd])
            acc = term if acc is None else acc + term
        parts.append(acc)
    attn = jnp.concatenate(parts, axis=-1)
    return _rms(attn, g_ref[...]).astype(_BF16)


def _attn_prompt_kernel(q_ref, kv_ref, w_uv_ref, g_ref, o_ref, s_ref, s_next_ref, m_ref, l_ref, acc_ref, *, tk):
    tq = q_ref.shape[2]
    rows = N_HEADS * tq
    qi = pl.program_id(1)
    m_ref[...] = jnp.full(m_ref.shape, -jnp.inf, _F32)
    l_ref[...] = jnp.zeros(l_ref.shape, _F32)
    acc_ref[...] = jnp.zeros(acc_ref.shape, _F32)
    n_full = (qi * tq) // tk

    def keys(c):
        return kv_ref[0, pl.ds(pl.multiple_of(c * tk, tk), tk), :]

    def scores(c):
        return _dot_nt(q_ref[0].reshape(rows, D_QK_PAD), keys(c))

    def softmax_pv(s, v):
        width = s.shape[1]
        m_prev = m_ref[...]
        m_new = jnp.maximum(m_prev, jnp.max(s, axis=-1, keepdims=True))
        p = jnp.exp2((s - jnp.tile(m_new, (1, width // LANE))) * EXP2_SCALE)
        alpha = jnp.exp2((m_prev - m_new) * EXP2_SCALE)
        m_ref[...] = m_new
        l_ref[...] = alpha * l_ref[...] + jnp.sum(p, axis=-1, keepdims=True)
        acc_ref[...] = jnp.tile(alpha, (1, D_C // LANE)) * acc_ref[...] + _dot(p.astype(_BF16), v)

    s_ref[...] = scores(0)

    def body(c, carry):
        s_next_ref[...] = scores(c + 1)
        softmax_pv(s_ref[...], keys(c)[:, :D_C])
        s_ref[...] = s_next_ref[...]
        return carry

    lax.fori_loop(0, n_full, body, 0)
    for r in range(tk // tq):
        @pl.when(qi % (tk // tq) == r)
        def _():
            width = (r + 1) * tq
            s = s_ref[:, :width]
            t_idx = r * tq + lax.broadcasted_iota(jnp.int32, (rows, width), 0) % tq
            k_idx = lax.broadcasted_iota(jnp.int32, (rows, width), 1)
            s = jnp.where(k_idx <= t_idx, s, -jnp.inf)
            softmax_pv(s, keys(n_full)[:width, :D_C])

    o_ref[...] = _attn_out(acc_ref[...] / jnp.tile(l_ref[...], (1, D_C // LANE)), tq, w_uv_ref, g_ref)


def _attn_prompt(q, kvb, w_uv, g, *, batch, seq, tk):
    tq = q.shape[2]
    nq = seq // tq
    rows = N_HEADS * tq
    return pl.pallas_call(
        functools.partial(_attn_prompt_kernel, tk=tk),
        grid=(batch, nq),
        in_specs=[pl.BlockSpec((1, N_HEADS, tq, D_QK_PAD), lambda b, i: (b * nq + i, 0, 0, 0)),
                  pl.BlockSpec((1, seq, D_QK_PAD), lambda b, i: (b, 0, 0)),
                  _const_spec(w_uv.shape), _const_spec(g.shape)],
        out_specs=pl.BlockSpec((tq, D_ATT), lambda b, i: (b * nq + i, 0)),
        out_shape=jax.ShapeDtypeStruct((batch * seq, D_ATT), _BF16),
        scratch_shapes=[pltpu.VMEM((rows, tk), _F32), pltpu.VMEM((rows, tk), _F32),
                        pltpu.VMEM((rows, LANE), _F32), pltpu.VMEM((rows, LANE), _F32),
                        pltpu.VMEM((rows, D_C), _F32)],
        compiler_params=pltpu.CompilerParams(dimension_semantics=("parallel", "parallel"), vmem_limit_bytes=VMEM_LIMIT),
        name="attn_prompt",
    )(q, kvb.reshape(batch, seq, D_QK_PAD), w_uv, g)


def _lanes_from_rows(col):
    rows = col.shape[0]
    sq = jnp.concatenate([jnp.broadcast_to(col, (rows, LANE)), jnp.zeros((LANE - rows, LANE), _F32)], axis=0)
    return sq.T[:, :rows]


def _attn_sample_kernel(pt_ref, q_ref, kvn_ref, w_uv_ref, g_ref, cache_ref, o_ref, pages_ref, sem_ref, kt_ref,
                        kn_ref, m_ref, l_ref, acct_ref, *, steps, n_pages):
    b = pl.program_id(0)
    rows = q_ref.shape[1]
    n_chunks = n_pages // PAGES_PER_STEP

    def page_copy(seq, chunk, p, slot):
        page = pt_ref[seq * n_pages + chunk * PAGES_PER_STEP + p]
        return pltpu.make_async_copy(cache_ref.at[page], pages_ref.at[slot, p], sem_ref.at[slot])

    def start_chunk(seq, chunk, slot):
        for p in range(PAGES_PER_STEP):
            page_copy(seq, chunk, p, slot).start()

    def wait_chunk(seq, chunk, slot):
        for p in range(PAGES_PER_STEP):
            page_copy(seq, chunk, p, slot).wait()

    @pl.when(b == 0)
    def _():
        start_chunk(0, 0, 0)

    kt_ref[D_QK:, :] = jnp.zeros((D_QK_PAD - D_QK, kt_ref.shape[1]), _BF16)
    m_ref[...] = jnp.full(m_ref.shape, -jnp.inf, _F32)
    l_ref[...] = jnp.zeros(l_ref.shape, _F32)
    acct_ref[...] = jnp.zeros(acct_ref.shape, _F32)

    for c in range(n_chunks):
        slot = c % 2
        if c + 1 < n_chunks:
            start_chunk(b, c + 1, 1 - slot)
        else:
            @pl.when(b + 1 < pl.num_programs(0))
            def _():
                start_chunk(b + 1, 0, 1 - slot)
        wait_chunk(b, c, slot)
        for p in range(PAGES_PER_STEP):
            kt_ref[:D_QK, p * PAGE_SIZE:(p + 1) * PAGE_SIZE] = pages_ref[slot, p].astype(_BF16)
        s = _dot(q_ref[0], kt_ref[...]) * ATTN_SCALE
        m_prev = m_ref[...]
        m_new = jnp.maximum(m_prev, jnp.max(s, axis=-1, keepdims=True))
        alpha = jnp.exp(m_prev - m_new)
        p_un = jnp.exp(s - m_new)
        m_ref[...] = m_new
        l_ref[...] = alpha * l_ref[...] + jnp.sum(p_un, axis=-1, keepdims=True)
        alpha_t = _lanes_from_rows(alpha)
        acct_ref[...] = (jnp.concatenate([alpha_t] * (D_C // LANE), axis=0) * acct_ref[...]
                         + _dot_nt(kt_ref[:D_C, :], p_un.astype(_BF16)))

    kn_ref[...] = jnp.zeros(kn_ref.shape, _BF16)
    kn_ref[:steps, :] = kvn_ref[0]
    kn = kn_ref[...]
    sn = _dot_nt(q_ref[0], kn) * ATTN_SCALE
    t_pos = lax.broadcasted_iota(jnp.int32, (rows, PAGE_SIZE), 0) % steps
    j_pos = lax.broadcasted_iota(jnp.int32, (rows, PAGE_SIZE), 1)
    sn = jnp.where(j_pos <= t_pos, sn, -jnp.inf)
    acc = jnp.concatenate([acct_ref[...], jnp.zeros((D_C, LANE - rows), _F32)], axis=1).T[:rows]
    _, l2, acc2 = _softmax_step(sn, m_ref[...], l_ref[...], acc, kn[:, :D_C])
    o_ref[0] = _attn_out(acc2 / l2, steps, w_uv_ref, g_ref)


def _attn_sample(page_table, q, kv_new, cache_t, w_uv, g):
    nb, rows, _ = q.shape
    steps = kv_new.shape[1]
    n_pages = page_table.shape[1]
    const = lambda shape: pl.BlockSpec(shape, lambda b, pt: (0,) * len(shape), pipeline_mode=pl.Buffered(1))
    return pl.pallas_call(
        functools.partial(_attn_sample_kernel, steps=steps, n_pages=n_pages),
        grid_spec=pltpu.PrefetchScalarGridSpec(
            num_scalar_prefetch=1,
            grid=(nb,),
            in_specs=[pl.BlockSpec((1, rows, D_QK_PAD), lambda b, pt: (b, 0, 0)),
                      pl.BlockSpec((1, steps, D_QK_PAD), lambda b, pt: (b, 0, 0)),
                      const(w_uv.shape), const(g.shape), pl.BlockSpec(memory_space=pl.ANY)],
            out_specs=pl.BlockSpec((1, steps, D_ATT), lambda b, pt: (b, 0, 0)),
            scratch_shapes=[pltpu.VMEM((2, PAGES_PER_STEP, D_QK, PAGE_SIZE), _F32),
                            pltpu.SemaphoreType.DMA((2,)),
                            pltpu.VMEM((D_QK_PAD, PAGES_PER_STEP * PAGE_SIZE), _BF16),
                            pltpu.VMEM((PAGE_SIZE, D_QK_PAD), _BF16),
                            pltpu.VMEM((rows, 1), _F32), pltpu.VMEM((rows, 1), _F32),
                            pltpu.VMEM((D_C, rows), _F32)]),
        out_shape=jax.ShapeDtypeStruct((nb, steps, D_ATT), _BF16),
        compiler_params=pltpu.CompilerParams(dimension_semantics=("arbitrary",), vmem_limit_bytes=VMEM_LIMIT),
        name="attn_sample",
    )(page_table.reshape(-1), q, kv_new, w_uv, g, cache_t)


def _out_ffn_kernel(x_ref, attn_ref, conv_ref, prev_g_ref, prev_v_ref, w_oa_ref, w_oc_ref, g_ffn_ref,
                    w_up_g_ref, w_up_v_ref, w_dw_g_ref, w_dw_v_ref, b_dw_g_ref, b_dw_v_ref, w_down_ref, g_fin_ref,
                    y_ref, tail_g_ref, tail_v_ref, ubuf_ref, carry_ref, acc_ref, h2_ref, *, shift, tiles_per_seq):
    gb, sb, d_model = x_ref.shape
    tm = gb * sb
    hist = prev_g_ref.shape[1]
    j = pl.program_id(1)
    n_chunks = pl.num_programs(1)
    first = pl.program_id(0) % tiles_per_seq == 0

    @pl.when(j == 0)
    def _():
        flat = lambda ref: ref[...].reshape(tm, ref.shape[2])
        acc_ref[...] = flat(x_ref) + _dot(flat(attn_ref), w_oa_ref[...]) + _dot(flat(conv_ref), w_oc_ref[...])
        h2_ref[...] = _rms(acc_ref[...], g_ffn_ref[...]).astype(_BF16)

    halves = []
    for part, (prev_ref, w_up_ref, w_dw_ref, b_dw_ref, tail_ref) in enumerate((
            (prev_g_ref, w_up_g_ref, w_dw_g_ref, b_dw_g_ref, tail_g_ref),
            (prev_v_ref, w_up_v_ref, w_dw_v_ref, b_dw_v_ref, tail_v_ref))):
        slot = part * n_chunks + j

        @pl.when(first)
        def _():
            ubuf_ref[part, :hist] = prev_ref[0]

        @pl.when(jnp.logical_not(first))
        def _():
            ubuf_ref[part, :hist] = carry_ref[slot]

        ubuf_ref[part, hist:] = _dot(h2_ref[...], w_up_ref[...])
        last_rows = ubuf_ref[part, tm:]
        carry_ref[slot] = last_rows
        tail_ref[0] = last_rows
        halves.append(b_dw_ref[...]
                      + w_dw_ref[0:1] * ubuf_ref[part, hist - 2 * shift:hist - 2 * shift + tm]
                      + w_dw_ref[1:2] * ubuf_ref[part, hist - shift:hist - shift + tm]
                      + w_dw_ref[2:3] * ubuf_ref[part, hist:])
    act = (halves[0] * jax.nn.sigmoid(halves[0]) * halves[1]).astype(_BF16)
    acc_ref[...] += _dot(act, w_down_ref[...])

    @pl.when(j == n_chunks - 1)
    def _():
        y_ref[...] = _rms(acc_ref[...], g_fin_ref[...]).reshape(gb, sb, d_model)


def _out_ffn(x, attn_n, conv_n, prev, wts, *, block, shift, tiles_per_seq):
    g_dim, s_dim, d_model = x.shape
    gb, sb = block
    n_s = s_dim // sb
    n_seq, hist, _ = prev.shape
    w_oa, w_oc, g_ffn, w_up, w_dw, b_dw, w_down, g_fin = wts
    fc, nj = FFN_CHUNK, FFN_CHUNKS
    row_spec = lambda w: pl.BlockSpec((gb, sb, w), lambda i, j: (i // n_s, i % n_s, 0))
    const = lambda w: pl.BlockSpec(w.shape, lambda i, j: (0,) * w.ndim, pipeline_mode=pl.Buffered(1))
    hist_spec = lambda part: pl.BlockSpec((1, hist, fc), lambda i, j: (i // tiles_per_seq, 0, part * nj + j))
    cols_spec = lambda rows, part: pl.BlockSpec((rows, fc), lambda i, j: (0, part * nj + j))
    tail_spec = pl.BlockSpec((1, hist, fc), lambda i, j: (i, 0, j))
    n_tiles = (g_dim // gb) * n_s
    return pl.pallas_call(
        functools.partial(_out_ffn_kernel, shift=shift, tiles_per_seq=tiles_per_seq),
        grid=(n_tiles, nj),
        in_specs=[row_spec(d_model), row_spec(D_ATT), row_spec(D_CONV), hist_spec(0), hist_spec(1),
                  const(w_oa), const(w_oc), const(g_ffn),
                  cols_spec(d_model, 0), cols_spec(d_model, 1), cols_spec(SUBLANES, 0), cols_spec(SUBLANES, 1),
                  cols_spec(1, 0), cols_spec(1, 1), pl.BlockSpec((fc, d_model), lambda i, j: (j, 0)), const(g_fin)],
        out_specs=[row_spec(d_model), tail_spec, tail_spec],
        out_shape=[jax.ShapeDtypeStruct(x.shape, _F32)] + [jax.ShapeDtypeStruct((n_tiles, hist, D_FF), _F32)] * 2,
        scratch_shapes=[pltpu.VMEM((2, hist + gb * sb, fc), _F32), pltpu.VMEM((2 * nj, hist, fc), _F32),
                        pltpu.VMEM((gb * sb, d_model), _F32), pltpu.VMEM((gb * sb, d_model), _BF16)],
        compiler_params=pltpu.CompilerParams(dimension_semantics=("arbitrary", "arbitrary"),
                                             vmem_limit_bytes=VMEM_LIMIT),
        name="out_ffn",
    )(x, attn_n, conv_n, prev, prev, w_oa, w_oc, g_ffn, w_up, w_up, w_dw, w_dw, b_dw, b_dw, w_down, g_fin)


def _rope_tables(pos):
    half = D_ROPE // 2
    inv = ROPE_THETA ** (-(jnp.arange(half, dtype=_F32) * 2.0 / D_ROPE))
    ang = pos.astype(_F32)[:, None] * inv[None, :]
    cos, sin = jnp.cos(ang), jnp.sin(ang)
    pad = ((0, 0), (0, LANE - D_ROPE))
    return jnp.pad(jnp.concatenate([cos, cos], axis=-1), pad), jnp.pad(jnp.concatenate([-sin, sin], axis=-1), pad)


def _swap_halves(w):
    half = D_ROPE // 2
    return jnp.concatenate([w[..., half:], w[..., :half]], axis=-1)


def _pad_lanes(w):
    return jnp.pad(w, [(0, 0)] * (w.ndim - 1) + [(0, LANE - w.shape[-1])])


def kernel(x_prompt, x_sample, cache_kv_latent, state_conv, state_ffn_conv, page_table, g_attn_norm, w_in, g_q_norm, w_uq, g_kv_norm, w_uk, w_uv, w_dw, b_dw, g_conv_ln, b_conv_ln, g_attn_out, g_conv_out, w_o, g_ffn_norm, w_up, w_ffn_dw, b_ffn_dw, w_down, g_final):
    batch, seq, d_model = x_prompt.shape
    nb, steps, _ = x_sample.shape
    n_pages = page_table.shape[1]
    past_len = n_pages * PAGE_SIZE
    assert w_in.shape[0] == 1, "single-layer trunk"
    assert seq % Q_TILE == 0 and nb % CONV_ROWS == 0 and steps <= PAGE_SIZE
    assert n_pages % (2 * PAGES_PER_STEP) == 0, "the two-slot page buffer needs an even number of chunks per sequence"

    row = lambda v: v.reshape(1, -1)
    o1, o2, o3 = D_CQ, D_CQ + D_C, D_CQ + D_C + D_ROPE
    wi = w_in[0]
    w_in_ext = jnp.concatenate([wi[:, :o2], _pad_lanes(wi[:, o2:o3]), _pad_lanes(_swap_halves(wi[:, o2:o3])),
                                wi[:, o3:]], axis=1).astype(_BF16)
    wq = w_uq[0]
    w_qn = wq[:, :, :D_NOPE].reshape(D_CQ, N_HEADS * D_NOPE).astype(_BF16)
    w_qr = _pad_lanes(wq[:, :, D_NOPE:]).reshape(D_CQ, N_HEADS * LANE).astype(_BF16)
    w_qs = _pad_lanes(_swap_halves(wq[:, :, D_NOPE:])).reshape(D_CQ, N_HEADS * LANE).astype(_BF16)
    wk = jnp.transpose(w_uk[0], (1, 2, 0))
    sel = (jnp.arange(HEAD_GROUP)[None, :] == (jnp.arange(N_HEADS) % HEAD_GROUP)[:, None]).astype(_F32)
    w_uk_z = (sel[:, :, None, None] * wk[:, None]).reshape(N_HEADS, HEAD_GROUP * D_NOPE, D_C).astype(_BF16)
    wv = jnp.transpose(w_uv[0], (1, 0, 2))
    sel2 = (jnp.arange(2)[None, :] == (jnp.arange(N_HEADS) % 2)[:, None]).astype(_F32)
    w_uv_z = (wv[:, :, None, :] * sel2[:, None, :, None]).reshape(N_HEADS, D_C, 2 * D_V).astype(_BF16)
    in_wts = (row(g_attn_norm[0]), w_in_ext, row(g_q_norm[0]), w_qn, w_qr, w_qs, row(g_kv_norm[0]), w_uk_z)
    conv_wts = (jnp.pad(w_dw[0], ((0, 1), (0, 0))), row(b_dw[0]), row(g_conv_ln[0]), row(b_conv_ln[0]),
                row(g_conv_out[0]))
    g_ao = row(g_attn_out[0])
    wo = w_o[0].astype(_BF16)
    w_fdw = jnp.pad(w_ffn_dw[0], ((0, SUBLANES - FFN_CONV_W), (0, 0)))
    ffn_wts = (wo[:D_ATT], wo[D_ATT:], row(g_ffn_norm[0]), w_up[0].astype(_BF16), w_fdw, row(b_ffn_dw[0]),
               w_down[0].astype(_BF16), row(g_final))

    rows_p = batch * seq
    tm = min(512, seq)
    xp = x_prompt.reshape(rows_p, d_model)
    cos_p, sin_p = _rope_tables(jnp.arange(seq, dtype=jnp.int32))
    q_p, kv_p, kvb_p, glu_p = _inproj(xp, cos_p, sin_p, in_wts, tm=tm, tq=Q_TILE)
    conv_p = _conv_prompt(glu_p, conv_wts, batch=batch, seq=seq, tc=tm)
    attn_p = _attn_prompt(q_p, kvb_p, w_uv_z, g_ao, batch=batch, seq=seq, tk=min(512, seq))
    hist_p = SUBLANES
    prev_p = jnp.zeros((batch, hist_p, 2 * D_FF), _F32)
    tiled = lambda v: v.reshape(rows_p // tm, tm, v.shape[-1])
    y_p, tail_g, tail_v = _out_ffn(tiled(xp), tiled(attn_p), tiled(conv_p), prev_p, ffn_wts, block=(1, tm), shift=1,
                                   tiles_per_seq=seq // tm)
    last = seq // tm - 1
    tail_p = jnp.concatenate([tail_g[last::seq // tm], tail_v[last::seq // tm]], axis=-1)
    y_prompt = y_p.reshape(batch, seq, d_model)
    new_kv_prompt = kv_p.reshape(1, batch, seq, D_QK)
    new_conv_prompt = glu_p.reshape(batch, seq, D_CONV)[None, :, seq - (CONV_W - 1):]
    new_ffn_prompt = tail_p[None, :, hist_p - (FFN_CONV_W - 1):]

    rows_s = nb * steps
    xs = jnp.transpose(x_sample, (1, 0, 2)).reshape(rows_s, d_model)
    pos_s = past_len + jnp.arange(rows_s, dtype=jnp.int32) // nb
    cos_s, sin_s = _rope_tables(pos_s)
    tq_s = min(Q_TILE, nb)
    q_s, kv_s, kvb_s, glu_s = _inproj(xs, cos_s, sin_s, in_wts, tm=rows_s, tq=tq_s)
    q_s = q_s.reshape(steps, nb // tq_s, N_HEADS, tq_s, D_QK_PAD)
    q_s = jnp.transpose(q_s, (1, 3, 2, 0, 4)).reshape(nb, N_HEADS * steps, D_QK_PAD)
    kvn_s = jnp.transpose(kvb_s.reshape(steps, nb, D_QK_PAD), (1, 0, 2))
    ext = jnp.concatenate([jnp.transpose(state_conv[0], (1, 0, 2)), glu_s.reshape(steps, nb, D_CONV)], axis=0)
    conv_s = _conv_sample(ext, conv_wts, steps=steps).reshape(rows_s, D_CONV)
    attn_s = _attn_sample(page_table, q_s, kvn_s, jnp.swapaxes(cache_kv_latent[0], 1, 2), w_uv_z, g_ao)
    attn_s = jnp.transpose(attn_s, (1, 0, 2)).reshape(rows_s, D_ATT)
    sb = min(SAMPLE_FFN_SEQS, nb)
    n_tiles = nb // sb
    hist_s = (FFN_CONV_W - 1) * sb
    prev_s = jnp.transpose(state_ffn_conv[0].reshape(n_tiles, sb, FFN_CONV_W - 1, 2 * D_FF), (0, 2, 1, 3))
    prev_s = prev_s.reshape(n_tiles, hist_s, 2 * D_FF)
    steps3 = lambda v: v.reshape(steps, nb, v.shape[-1])
    y_s, tail_g, tail_v = _out_ffn(steps3(xs), steps3(attn_s), steps3(conv_s), prev_s, ffn_wts, block=(steps, sb),
                                   shift=sb, tiles_per_seq=1)
    tail_s = jnp.concatenate([tail_g, tail_v], axis=-1)
    y_sample = jnp.transpose(y_s, (1, 0, 2))
    new_kv_sample = jnp.transpose(kv_s.reshape(steps, nb, D_QK), (1, 0, 2))[None]
    new_conv_sample = jnp.transpose(ext[steps:], (1, 0, 2))[None]
    tail_s = tail_s.reshape(n_tiles, FFN_CONV_W - 1, sb, 2 * D_FF)
    new_ffn_sample = jnp.transpose(tail_s, (0, 2, 1, 3)).reshape(1, nb, FFN_CONV_W - 1, 2 * D_FF)
    return (y_prompt, y_sample, new_kv_prompt, new_conv_prompt, new_ffn_prompt,
            new_kv_sample, new_conv_sample, new_ffn_sample)
```

```python
import functools

import jax
import jax.numpy as jnp
from jax import lax
from jax.experimental import pallas as pl
from jax.experimental.pallas import tpu as pltpu

N_HEADS = 8
D_NOPE = 64
D_ROPE = 32
D_V = 64
D_C = 256
D_CQ = 384
D_ATT = N_HEADS * D_V
D_CONV = 512
CONV_W = 31
D_FF = 2816
FFN_CONV_W = 3
ROPE_THETA = 10000.0
PAGE_SIZE = 128
EPS = 1e-6
ATTN_SCALE = (D_NOPE + D_ROPE) ** -0.5
EXP2_SCALE = ATTN_SCALE * 1.4426950408889634

LANE = 128
SUBLANES = 8
D_QK = D_C + D_ROPE
D_QK_PAD = D_C + LANE
Q_TILE = 128
HEAD_GROUP = 256 // D_NOPE
HEAD_PAD = LANE
FFN_CHUNKS = 1
FFN_CHUNK = D_FF // FFN_CHUNKS
PAGES_PER_STEP = 64
SAMPLE_FFN_SEQS = 32
VMEM_LIMIT = 56 * 1024 * 1024

_BF16 = jnp.bfloat16
_F32 = jnp.float32


def _rms(x, g):
    return x * lax.rsqrt(jnp.mean(x * x, axis=-1, keepdims=True) + EPS) * g


def _dot(a, b):
    return jnp.dot(a, b, preferred_element_type=_F32)


def _dot_nt(a, b):
    return lax.dot_general(a, b, (((1,), (1,)), ((), ())), preferred_element_type=_F32)


def _const_spec(shape):
    nd = len(shape)
    return pl.BlockSpec(shape, lambda *_: (0,) * nd, pipeline_mode=pl.Buffered(1))


def _inproj_kernel(x_ref, cos_ref, sin_ref, g_in_ref, w_in_ref, g_q_ref, w_qn_ref, w_qr_ref,
                   w_qs_ref, g_kv_ref, w_uk_ref, q_ref, kv_ref, kvb_ref, glu_ref, *, tq):
    tm = x_ref.shape[0]
    cos = cos_ref[...]
    sin = sin_ref[...]
    h = _rms(x_ref[...], g_in_ref[...]).astype(_BF16)
    proj = _dot(h, w_in_ref[...])
    o_kv, o_kr, o_ks, o_ga, o_gb = D_CQ, D_CQ + D_C, D_CQ + D_C + LANE, D_CQ + D_C + 2 * LANE, D_CQ + D_C + 2 * LANE + D_CONV
    cq = proj[:, :o_kv]
    ckv = proj[:, o_kv:o_kr]
    glu_ref[...] = proj[:, o_ga:o_gb] * jax.nn.sigmoid(proj[:, o_gb:])
    ckv_n = _rms(ckv, g_kv_ref[...])
    kr = proj[:, o_kr:o_ks] * cos + proj[:, o_ks:o_ga] * sin
    kv_ref[:, :D_C] = ckv_n
    kv_ref[:, D_C:] = kr[:, :D_ROPE]
    kvb_ref[:, :D_C] = ckv_n.astype(_BF16)
    kvb_ref[:, D_C:] = kr.astype(_BF16)
    cqn = _rms(cq, g_q_ref[...]).astype(_BF16)
    qn = _dot(cqn, w_qn_ref[...]).astype(_BF16)
    qr = _dot(cqn, w_qr_ref[...])
    qs = _dot(cqn, w_qs_ref[...])
    for hd in range(N_HEADS):
        grp = hd // HEAD_GROUP
        ql = _dot(qn[:, grp * 256:(grp + 1) * 256], w_uk_ref[hd]).astype(_BF16)
        qrh = (qr[:, hd * LANE:(hd + 1) * LANE] * cos + qs[:, hd * LANE:(hd + 1) * LANE] * sin).astype(_BF16)
        for s in range(tm // tq):
            q_ref[s, hd, :, :D_C] = ql[s * tq:(s + 1) * tq]
            q_ref[s, hd, :, D_C:] = qrh[s * tq:(s + 1) * tq]


def _inproj(x, cos_tab, sin_tab, wts, *, tm, tq):
    rows = x.shape[0]
    n_tab = cos_tab.shape[0] // tm
    g_in, w_in, g_q, w_qn, w_qr, w_qs, g_kv, w_uk = wts
    row_spec = lambda w: pl.BlockSpec((tm, w), lambda i: (i, 0))
    tab_spec = pl.BlockSpec((tm, LANE), lambda i: (i % n_tab, 0))
    return pl.pallas_call(
        functools.partial(_inproj_kernel, tq=tq),
        grid=(rows // tm,),
        in_specs=[row_spec(x.shape[1]), tab_spec, tab_spec] + [_const_spec(w.shape) for w in wts],
        out_specs=[pl.BlockSpec((tm // tq, N_HEADS, tq, D_QK_PAD), lambda i: (i, 0, 0, 0)),
                   row_spec(D_QK), row_spec(D_QK_PAD), row_spec(D_CONV)],
        out_shape=[jax.ShapeDtypeStruct((rows // tq, N_HEADS, tq, D_QK_PAD), _BF16),
                   jax.ShapeDtypeStruct((rows, D_QK), _F32),
                   jax.ShapeDtypeStruct((rows, D_QK_PAD), _BF16),
                   jax.ShapeDtypeStruct((rows, D_CONV), _F32)],
        compiler_params=pltpu.CompilerParams(dimension_semantics=("parallel",), vmem_limit_bytes=VMEM_LIMIT),
        name="inproj",
    )(x, cos_tab, sin_tab, *wts)


def _inproj_heads_kernel(x_ref, cos_ref, sin_ref, g_in_ref, w_in_ref, g_q_ref, w_qa_ref, w_qb_ref, g_kv_ref,
                         w_kn_ref, q_ref, kt_ref, v_ref, kv_ref, glu_ref, *, tq):
    tm = x_ref.shape[0]
    cos = cos_ref[...]
    sin = sin_ref[...]
    h = _rms(x_ref[...], g_in_ref[...]).astype(_BF16)
    proj = _dot(h, w_in_ref[...])
    o_kv, o_kr, o_ks, o_ga, o_gb = D_CQ, D_CQ + D_C, D_CQ + D_C + LANE, D_CQ + D_C + 2 * LANE, D_CQ + D_C + 2 * LANE + D_CONV
    glu_ref[...] = proj[:, o_ga:o_gb] * jax.nn.sigmoid(proj[:, o_gb:])
    ckv_n = _rms(proj[:, o_kv:o_kr], g_kv_ref[...])
    kr = proj[:, o_kr:o_ks] * cos + proj[:, o_ks:o_ga] * sin
    kv_ref[:, :D_C] = ckv_n
    kv_ref[:, D_C:] = kr[:, D_NOPE:D_NOPE + D_ROPE]
    ckv_b = ckv_n.astype(_BF16)
    v_ref[...] = ckv_b
    k_nope = _dot(ckv_b, w_kn_ref[...])
    cqn = _rms(proj[:, :o_kv], g_q_ref[...]).astype(_BF16)
    qa = _dot(cqn, w_qa_ref[...])
    qb = _dot(cqn, w_qb_ref[...])
    for hd in range(N_HEADS):
        blk = slice(hd * HEAD_PAD, (hd + 1) * HEAD_PAD)
        q_h = (qa[:, blk] * cos + qb[:, blk] * sin).astype(_BF16)
        for s in range(tm // tq):
            q_ref[s, hd] = q_h[s * tq:(s + 1) * tq]
        kt_ref[0, hd] = (k_nope[:, blk] + kr).T.astype(_BF16)


def _inproj_heads(x, cos_tab, sin_tab, wts, *, batch, tm, tq):
    rows = x.shape[0]
    seq = rows // batch
    n_tab = seq // tm
    row_spec = lambda w: pl.BlockSpec((tm, w), lambda i: (i, 0))
    tab_spec = pl.BlockSpec((tm, LANE), lambda i: (i % n_tab, 0))
    return pl.pallas_call(
        functools.partial(_inproj_heads_kernel, tq=tq),
        grid=(rows // tm,),
        in_specs=[row_spec(x.shape[1]), tab_spec, tab_spec] + [_const_spec(w.shape) for w in wts],
        out_specs=[pl.BlockSpec((tm // tq, N_HEADS, tq, HEAD_PAD), lambda i: (i, 0, 0, 0)),
                   pl.BlockSpec((1, N_HEADS, HEAD_PAD, tm), lambda i: (i // n_tab, 0, 0, i % n_tab)),
                   row_spec(D_C), row_spec(D_QK), row_spec(D_CONV)],
        out_shape=[jax.ShapeDtypeStruct((rows // tq, N_HEADS, tq, HEAD_PAD), _BF16),
                   jax.ShapeDtypeStruct((batch, N_HEADS, HEAD_PAD, seq), _BF16),
                   jax.ShapeDtypeStruct((rows, D_C), _BF16),
                   jax.ShapeDtypeStruct((rows, D_QK), _F32),
                   jax.ShapeDtypeStruct((rows, D_CONV), _F32)],
        compiler_params=pltpu.CompilerParams(dimension_semantics=("parallel",), vmem_limit_bytes=VMEM_LIMIT),
        name="inproj_heads",
    )(x, cos_tab, sin_tab, *wts)


CONV_HALO = 32
CONV_ROWS = 16
CONV_UNROLL = 8


def _conv_post(y, g_ln, b_ln, g_out):
    mu = jnp.mean(y, axis=-1, keepdims=True)
    yc = y - mu
    z = yc * lax.rsqrt(jnp.mean(yc * yc, axis=-1, keepdims=True) + EPS) * g_ln + b_ln
    return _rms(z * jax.nn.sigmoid(z), g_out).astype(_BF16)


def _conv_prompt_kernel(cur_ref, halo_ref, w_ref, b_ref, g_ln_ref, b_ln_ref, g_out_ref, o_ref, ext_ref, sh_ref):
    tc = cur_ref.shape[1]
    first = pl.program_id(1) == 0
    ext_ref[:CONV_HALO] = jnp.where(first, 0.0, halo_ref[0])
    ext_ref[CONV_HALO:CONV_HALO + tc] = cur_ref[0]
    ext_ref[CONV_HALO + tc:] = jnp.zeros((SUBLANES, D_CONV), _F32)
    off = CONV_HALO - (CONV_W - 1)
    for r in range(SUBLANES):
        sh_ref[r] = ext_ref[r:r + sh_ref.shape[1], :]

    def body(it, carry):
        for u in range(CONV_UNROLL):
            base = pl.multiple_of((it * CONV_UNROLL + u) * CONV_ROWS, CONV_ROWS)
            acc = jnp.broadcast_to(b_ref[...], (CONV_ROWS, D_CONV))
            for k in range(CONV_W):
                r = (off + k) % SUBLANES
                acc = acc + w_ref[k:k + 1, :] * sh_ref[r, pl.ds(base + (off + k - r), CONV_ROWS), :]
            o_ref[pl.ds(base, CONV_ROWS), :] = _conv_post(acc, g_ln_ref[...], b_ln_ref[...], g_out_ref[...])
        return carry

    lax.fori_loop(0, tc // (CONV_ROWS * CONV_UNROLL), body, 0)


def _conv_prompt(glu, wts, *, batch, seq, tc):
    glu3 = glu.reshape(batch, seq, D_CONV)
    hb = tc // CONV_HALO
    return pl.pallas_call(
        _conv_prompt_kernel,
        grid=(batch, seq // tc),
        in_specs=[pl.BlockSpec((1, tc, D_CONV), lambda b, i: (b, i, 0)),
                  pl.BlockSpec((1, CONV_HALO, D_CONV), lambda b, i: (b, jnp.maximum(i * hb - 1, 0), 0))]
                 + [_const_spec(w.shape) for w in wts],
        out_specs=pl.BlockSpec((tc, D_CONV), lambda b, i: (b * (seq // tc) + i, 0)),
        out_shape=jax.ShapeDtypeStruct((batch * seq, D_CONV), _BF16),
        scratch_shapes=[pltpu.VMEM((CONV_HALO + tc + SUBLANES, D_CONV), _F32),
                        pltpu.VMEM((SUBLANES, CONV_HALO + tc, D_CONV), _F32)],
        compiler_params=pltpu.CompilerParams(dimension_semantics=("parallel", "parallel"), vmem_limit_bytes=VMEM_LIMIT),
        name="conv_prompt",
    )(glu3, glu3, *wts)


def _conv_sample_kernel(ext_ref, w_ref, b_ref, g_ln_ref, b_ln_ref, g_out_ref, o_ref):
    steps, nb = o_ref.shape[0], o_ref.shape[1]
    chunks = nb // CONV_ROWS

    def body(it, carry):
        t = it // chunks
        base = pl.multiple_of((it % chunks) * CONV_ROWS, CONV_ROWS)
        acc = jnp.broadcast_to(b_ref[...], (CONV_ROWS, D_CONV))
        for k in range(CONV_W):
            acc = acc + w_ref[k:k + 1, :] * ext_ref[t + k, pl.ds(base, CONV_ROWS), :]
        o_ref[t, pl.ds(base, CONV_ROWS), :] = _conv_post(acc, g_ln_ref[...], b_ln_ref[...], g_out_ref[...])
        return carry

    lax.fori_loop(0, steps * chunks, body, 0)


def _conv_sample(ext, wts, *, steps):
    nb = ext.shape[1]
    return pl.pallas_call(
        _conv_sample_kernel,
        out_shape=jax.ShapeDtypeStruct((steps, nb, D_CONV), _BF16),
        compiler_params=pltpu.CompilerParams(vmem_limit_bytes=VMEM_LIMIT),
        name="conv_sample",
    )(ext, *wts)


def _softmax_step(s, m_prev, l_prev, acc_prev, v):
    m_new = jnp.maximum(m_prev, jnp.max(s, axis=-1, keepdims=True))
    alpha = jnp.exp(m_prev - m_new)
    p = jnp.exp(s - m_new)
    l_new = alpha * l_prev + jnp.sum(p, axis=-1, keepdims=True)
    acc_new = alpha * acc_prev + _dot(p.astype(_BF16), v)
    return m_new, l_new, acc_new


def _attn_out(o, rows_per_head, w_uv_ref, g_ref):
    parts = []
    for pr in range(N_HEADS // 2):
        acc = None
        for hd in (2 * pr, 2 * pr + 1):
            oh = o[hd * rows_per_head:(hd + 1) * rows_per_head].astype(_BF16)
            term = _dot(oh, w_uv_ref[hd])
            acc = term if acc is None else acc + term
        parts.append(acc)
    attn = jnp.concatenate(parts, axis=-1)
    return _rms(attn, g_ref[...]).astype(_BF16)


def _attn_prompt_kernel(q_ref, kt_ref, v_ref, w_uv_ref, g_ref, o_ref, s_ref, s_next_ref, m_ref, l_ref, acc_ref,
                        *, tk):
    tq = q_ref.shape[2]
    rows = N_HEADS * tq
    qi = pl.program_id(1)
    m_ref[...] = jnp.full(m_ref.shape, -jnp.inf, _F32)
    l_ref[...] = jnp.zeros(l_ref.shape, _F32)
    acc_ref[...] = jnp.zeros(acc_ref.shape, _F32)
    n_full = (qi * tq) // tk

    def values(c):
        return v_ref[0, pl.ds(pl.multiple_of(c * tk, tk), tk), :]

    def scores_into(dst_ref, c):
        start = pl.multiple_of(c * tk, tk)
        for hd in range(N_HEADS):
            dst_ref[hd * tq:(hd + 1) * tq, :] = _dot(q_ref[0, hd], kt_ref[0, hd, :, pl.ds(start, tk)])

    def softmax_pv(s, v):
        width = s.shape[1]
        m_prev = m_ref[...]
        m_new = jnp.maximum(m_prev, jnp.max(s, axis=-1, keepdims=True))
        p = jnp.exp2((s - jnp.tile(m_new, (1, width // LANE))) * EXP2_SCALE)
        alpha = jnp.exp2((m_prev - m_new) * EXP2_SCALE)
        m_ref[...] = m_new
        l_ref[...] = alpha * l_ref[...] + jnp.sum(p, axis=-1, keepdims=True)
        acc_ref[...] = jnp.tile(alpha, (1, D_C // LANE)) * acc_ref[...] + _dot(p.astype(_BF16), v)

    scores_into(s_ref, 0)

    def pair(i, carry):
        c = 2 * i
        scores_into(s_next_ref, c + 1)
        softmax_pv(s_ref[...], values(c))
        scores_into(s_ref, c + 2)
        softmax_pv(s_next_ref[...], values(c + 1))
        return carry

    lax.fori_loop(0, n_full // 2, pair, 0)

    @pl.when(n_full % 2 == 1)
    def _():
        scores_into(s_next_ref, n_full)
        softmax_pv(s_ref[...], values(n_full - 1))
        s_ref[...] = s_next_ref[...]

    for r in range(tk // tq):
        @pl.when(qi % (tk // tq) == r)
        def _():
            width = (r + 1) * tq
            s = s_ref[:, :width]
            t_idx = r * tq + lax.broadcasted_iota(jnp.int32, (rows, width), 0) % tq
            k_idx = lax.broadcasted_iota(jnp.int32, (rows, width), 1)
            s = jnp.where(k_idx <= t_idx, s, -jnp.inf)
            softmax_pv(s, values(n_full)[:width])

    o_ref[...] = _attn_out(acc_ref[...] / jnp.tile(l_ref[...], (1, D_C // LANE)), tq, w_uv_ref, g_ref)


def _attn_prompt(q, kt, v, w_uv, g, *, batch, seq, tk):
    tq = q.shape[2]
    nq = seq // tq
    rows = N_HEADS * tq
    return pl.pallas_call(
        functools.partial(_attn_prompt_kernel, tk=tk),
        grid=(batch, nq),
        in_specs=[pl.BlockSpec((1, N_HEADS, tq, HEAD_PAD), lambda b, i: (b * nq + i, 0, 0, 0)),
                  pl.BlockSpec((1, N_HEADS, HEAD_PAD, seq), lambda b, i: (b, 0, 0, 0)),
                  pl.BlockSpec((1, seq, D_C), lambda b, i: (b, 0, 0)),
                  _const_spec(w_uv.shape), _const_spec(g.shape)],
        out_specs=pl.BlockSpec((tq, D_ATT), lambda b, i: (b * nq + i, 0)),
        out_shape=jax.ShapeDtypeStruct((batch * seq, D_ATT), _BF16),
        scratch_shapes=[pltpu.VMEM((rows, tk), _F32), pltpu.VMEM((rows, tk), _F32),
                        pltpu.VMEM((rows, LANE), _F32), pltpu.VMEM((rows, LANE), _F32),
                        pltpu.VMEM((rows, D_C), _F32)],
        compiler_params=pltpu.CompilerParams(dimension_semantics=("parallel", "parallel"), vmem_limit_bytes=VMEM_LIMIT),
        name="attn_prompt",
    )(q, kt, v.reshape(batch, seq, D_C), w_uv, g)


def _lanes_from_rows(col):
    rows = col.shape[0]
    sq = jnp.concatenate([jnp.broadcast_to(col, (rows, LANE)), jnp.zeros((LANE - rows, LANE), _F32)], axis=0)
    return sq.T[:, :rows]


def _attn_sample_kernel(pt_ref, q_ref, kvn_ref, w_uv_ref, g_ref, cache_ref, o_ref, pages_ref, sem_ref, kt_ref,
                        kn_ref, m_ref, l_ref, acct_ref, *, steps, n_pages):
    b = pl.program_id(0)
    rows = q_ref.shape[1]
    n_chunks = n_pages // PAGES_PER_STEP

    def page_copy(seq, chunk, p, slot):
        page = pt_ref[seq * n_pages + chunk * PAGES_PER_STEP + p]
        return pltpu.make_async_copy(cache_ref.at[page], pages_ref.at[slot, p], sem_ref.at[slot])

    def start_chunk(seq, chunk, slot):
        for p in range(PAGES_PER_STEP):
            page_copy(seq, chunk, p, slot).start()

    def wait_chunk(seq, chunk, slot):
        for p in range(PAGES_PER_STEP):
            page_copy(seq, chunk, p, slot).wait()

    @pl.when(b == 0)
    def _():
        start_chunk(0, 0, 0)

    kt_ref[D_QK:, :] = jnp.zeros((D_QK_PAD - D_QK, kt_ref.shape[1]), _BF16)
    m_ref[...] = jnp.full(m_ref.shape, -jnp.inf, _F32)
    l_ref[...] = jnp.zeros(l_ref.shape, _F32)
    acct_ref[...] = jnp.zeros(acct_ref.shape, _F32)

    for c in range(n_chunks):
        slot = c % 2
        if c + 1 < n_chunks:
            start_chunk(b, c + 1, 1 - slot)
        else:
            @pl.when(b + 1 < pl.num_programs(0))
            def _():
                start_chunk(b + 1, 0, 1 - slot)
        wait_chunk(b, c, slot)
        for p in range(PAGES_PER_STEP):
            kt_ref[:D_QK, p * PAGE_SIZE:(p + 1) * PAGE_SIZE] = pages_ref[slot, p].astype(_BF16)
        s = _dot(q_ref[0], kt_ref[...]) * ATTN_SCALE
        m_prev = m_ref[...]
        m_new = jnp.maximum(m_prev, jnp.max(s, axis=-1, keepdims=True))
        alpha = jnp.exp(m_prev - m_new)
        p_un = jnp.exp(s - m_new)
        m_ref[...] = m_new
        l_ref[...] = alpha * l_ref[...] + jnp.sum(p_un, axis=-1, keepdims=True)
        alpha_t = _lanes_from_rows(alpha)
        acct_ref[...] = (jnp.concatenate([alpha_t] * (D_C // LANE), axis=0) * acct_ref[...]
                         + _dot_nt(kt_ref[:D_C, :], p_un.astype(_BF16)))

    kn_ref[...] = jnp.zeros(kn_ref.shape, _BF16)
    kn_ref[:steps, :] = kvn_ref[0]
    kn = kn_ref[...]
    sn = _dot_nt(q_ref[0], kn) * ATTN_SCALE
    t_pos = lax.broadcasted_iota(jnp.int32, (rows, PAGE_SIZE), 0) % steps
    j_pos = lax.broadcasted_iota(jnp.int32, (rows, PAGE_SIZE), 1)
    sn = jnp.where(j_pos <= t_pos, sn, -jnp.inf)
    acc = jnp.concatenate([acct_ref[...], jnp.zeros((D_C, LANE - rows), _F32)], axis=1).T[:rows]
    _, l2, acc2 = _softmax_step(sn, m_ref[...], l_ref[...], acc, kn[:, :D_C])
    o_ref[0] = _attn_out(acc2 / l2, steps, w_uv_ref, g_ref)


def _attn_sample(page_table, q, kv_new, cache_t, w_uv, g):
    nb, rows, _ = q.shape
    steps = kv_new.shape[1]
    n_pages = page_table.shape[1]
    const = lambda shape: pl.BlockSpec(shape, lambda b, pt: (0,) * len(shape), pipeline_mode=pl.Buffered(1))
    return pl.pallas_call(
        functools.partial(_attn_sample_kernel, steps=steps, n_pages=n_pages),
        grid_spec=pltpu.PrefetchScalarGridSpec(
            num_scalar_prefetch=1,
            grid=(nb,),
            in_specs=[pl.BlockSpec((1, rows, D_QK_PAD), lambda b, pt: (b, 0, 0)),
                      pl.BlockSpec((1, steps, D_QK_PAD), lambda b, pt: (b, 0, 0)),
                      const(w_uv.shape), const(g.shape), pl.BlockSpec(memory_space=pl.ANY)],
            out_specs=pl.BlockSpec((1, steps, D_ATT), lambda b, pt: (b, 0, 0)),
            scratch_shapes=[pltpu.VMEM((2, PAGES_PER_STEP, D_QK, PAGE_SIZE), _F32),
                            pltpu.SemaphoreType.DMA((2,)),
                            pltpu.VMEM((D_QK_PAD, PAGES_PER_STEP * PAGE_SIZE), _BF16),
                            pltpu.VMEM((PAGE_SIZE, D_QK_PAD), _BF16),
                            pltpu.VMEM((rows, 1), _F32), pltpu.VMEM((rows, 1), _F32),
                            pltpu.VMEM((D_C, rows), _F32)]),
        out_shape=jax.ShapeDtypeStruct((nb, steps, D_ATT), _BF16),
        compiler_params=pltpu.CompilerParams(dimension_semantics=("arbitrary",), vmem_limit_bytes=VMEM_LIMIT),
        name="attn_sample",
    )(page_table.reshape(-1), q, kv_new, w_uv, g, cache_t)


def _out_ffn_kernel(x_ref, attn_ref, conv_ref, prev_g_ref, prev_v_ref, w_oa_ref, w_oc_ref, g_ffn_ref,
                    w_up_g_ref, w_up_v_ref, w_dw_g_ref, w_dw_v_ref, b_dw_g_ref, b_dw_v_ref, w_down_ref, g_fin_ref,
                    y_ref, tail_g_ref, tail_v_ref, ubuf_ref, carry_ref, acc_ref, h2_ref, *, shift, tiles_per_seq):
    gb, sb, d_model = x_ref.shape
    tm = gb * sb
    hist = prev_g_ref.shape[1]
    j = pl.program_id(1)
    n_chunks = pl.num_programs(1)
    first = pl.program_id(0) % tiles_per_seq == 0

    @pl.when(j == 0)
    def _():
        flat = lambda ref: ref[...].reshape(tm, ref.shape[2])
        acc_ref[...] = flat(x_ref) + _dot(flat(attn_ref), w_oa_ref[...]) + _dot(flat(conv_ref), w_oc_ref[...])
        h2_ref[...] = _rms(acc_ref[...], g_ffn_ref[...]).astype(_BF16)

    halves = []
    for part, (prev_ref, w_up_ref, w_dw_ref, b_dw_ref, tail_ref) in enumerate((
            (prev_g_ref, w_up_g_ref, w_dw_g_ref, b_dw_g_ref, tail_g_ref),
            (prev_v_ref, w_up_v_ref, w_dw_v_ref, b_dw_v_ref, tail_v_ref))):
        slot = part * n_chunks + j

        @pl.when(first)
        def _():
            ubuf_ref[part, :hist] = prev_ref[0]

        @pl.when(jnp.logical_not(first))
        def _():
            ubuf_ref[part, :hist] = carry_ref[slot]

        ubuf_ref[part, hist:] = _dot(h2_ref[...], w_up_ref[...])
        last_rows = ubuf_ref[part, tm:]
        carry_ref[slot] = last_rows
        tail_ref[0] = last_rows
        halves.append(b_dw_ref[...]
                      + w_dw_ref[0:1] * ubuf_ref[part, hist - 2 * shift:hist - 2 * shift + tm]
                      + w_dw_ref[1:2] * ubuf_ref[part, hist - shift:hist - shift + tm]
                      + w_dw_ref[2:3] * ubuf_ref[part, hist:])
    act = (halves[0] * jax.nn.sigmoid(halves[0]) * halves[1]).astype(_BF16)
    acc_ref[...] += _dot(act, w_down_ref[...])

    @pl.when(j == n_chunks - 1)
    def _():
        y_ref[...] = _rms(acc_ref[...], g_fin_ref[...]).reshape(gb, sb, d_model)


def _out_ffn(x, attn_n, conv_n, prev, wts, *, block, shift, tiles_per_seq):
    g_dim, s_dim, d_model = x.shape
    gb, sb = block
    n_s = s_dim // sb
    n_seq, hist, _ = prev.shape
    w_oa, w_oc, g_ffn, w_up, w_dw, b_dw, w_down, g_fin = wts
    fc, nj = FFN_CHUNK, FFN_CHUNKS
    row_spec = lambda w: pl.BlockSpec((gb, sb, w), lambda i, j: (i // n_s, i % n_s, 0))
    const = lambda w: pl.BlockSpec(w.shape, lambda i, j: (0,) * w.ndim, pipeline_mode=pl.Buffered(1))
    hist_spec = lambda part: pl.BlockSpec((1, hist, fc), lambda i, j: (i // tiles_per_seq, 0, part * nj + j))
    cols_spec = lambda rows, part: pl.BlockSpec((rows, fc), lambda i, j: (0, part * nj + j))
    tail_spec = pl.BlockSpec((1, hist, fc), lambda i, j: (i, 0, j))
    n_tiles = (g_dim // gb) * n_s
    return pl.pallas_call(
        functools.partial(_out_ffn_kernel, shift=shift, tiles_per_seq=tiles_per_seq),
        grid=(n_tiles, nj),
        in_specs=[row_spec(d_model), row_spec(D_ATT), row_spec(D_CONV), hist_spec(0), hist_spec(1),
                  const(w_oa), const(w_oc), const(g_ffn),
                  cols_spec(d_model, 0), cols_spec(d_model, 1), cols_spec(SUBLANES, 0), cols_spec(SUBLANES, 1),
                  cols_spec(1, 0), cols_spec(1, 1), pl.BlockSpec((fc, d_model), lambda i, j: (j, 0)), const(g_fin)],
        out_specs=[row_spec(d_model), tail_spec, tail_spec],
        out_shape=[jax.ShapeDtypeStruct(x.shape, _F32)] + [jax.ShapeDtypeStruct((n_tiles, hist, D_FF), _F32)] * 2,
        scratch_shapes=[pltpu.VMEM((2, hist + gb * sb, fc), _F32), pltpu.VMEM((2 * nj, hist, fc), _F32),
                        pltpu.VMEM((gb * sb, d_model), _F32), pltpu.VMEM((gb * sb, d_model), _BF16)],
        compiler_params=pltpu.CompilerParams(dimension_semantics=("arbitrary", "arbitrary"),
                                             vmem_limit_bytes=VMEM_LIMIT),
        name="out_ffn",
    )(x, attn_n, conv_n, prev, prev, w_oa, w_oc, g_ffn, w_up, w_up, w_dw, w_dw, b_dw, b_dw, w_down, g_fin)


def _rope_tables(pos, lead=0):
    half = D_ROPE // 2
    inv = ROPE_THETA ** (-(jnp.arange(half, dtype=_F32) * 2.0 / D_ROPE))
    ang = pos.astype(_F32)[:, None] * inv[None, :]
    cos, sin = jnp.cos(ang), jnp.sin(ang)
    pad = ((0, 0), (0, LANE - D_ROPE - lead))
    one, zero = jnp.ones((pos.shape[0], lead), _F32), jnp.zeros((pos.shape[0], lead), _F32)
    return (jnp.pad(jnp.concatenate([one, cos, cos], axis=-1), pad),
            jnp.pad(jnp.concatenate([zero, -sin, sin], axis=-1), pad))


def _swap_halves(w):
    half = D_ROPE // 2
    return jnp.concatenate([w[..., half:], w[..., :half]], axis=-1)


def _pad_lanes(w, lead=0):
    return jnp.pad(w, [(0, 0)] * (w.ndim - 1) + [(lead, LANE - w.shape[-1] - lead)])


def kernel(x_prompt, x_sample, cache_kv_latent, state_conv, state_ffn_conv, page_table, g_attn_norm, w_in, g_q_norm, w_uq, g_kv_norm, w_uk, w_uv, w_dw, b_dw, g_conv_ln, b_conv_ln, g_attn_out, g_conv_out, w_o, g_ffn_norm, w_up, w_ffn_dw, b_ffn_dw, w_down, g_final):
    batch, seq, d_model = x_prompt.shape
    nb, steps, _ = x_sample.shape
    n_pages = page_table.shape[1]
    past_len = n_pages * PAGE_SIZE
    assert w_in.shape[0] == 1, "single-layer trunk"
    assert seq % Q_TILE == 0 and nb % CONV_ROWS == 0 and steps <= PAGE_SIZE
    assert n_pages % (2 * PAGES_PER_STEP) == 0, "the two-slot page buffer needs an even number of chunks per sequence"

    row = lambda v: v.reshape(1, -1)
    o1, o2, o3 = D_CQ, D_CQ + D_C, D_CQ + D_C + D_ROPE
    wi = w_in[0]
    w_in_ext = jnp.concatenate([wi[:, :o2], _pad_lanes(wi[:, o2:o3]), _pad_lanes(_swap_halves(wi[:, o2:o3])),
                                wi[:, o3:]], axis=1).astype(_BF16)
    wq = w_uq[0]
    w_qn = wq[:, :, :D_NOPE].reshape(D_CQ, N_HEADS * D_NOPE).astype(_BF16)
    w_qr = _pad_lanes(wq[:, :, D_NOPE:]).reshape(D_CQ, N_HEADS * LANE).astype(_BF16)
    w_qs = _pad_lanes(_swap_halves(wq[:, :, D_NOPE:])).reshape(D_CQ, N_HEADS * LANE).astype(_BF16)
    wk = jnp.transpose(w_uk[0], (1, 2, 0))
    sel = (jnp.arange(HEAD_GROUP)[None, :] == (jnp.arange(N_HEADS) % HEAD_GROUP)[:, None]).astype(_F32)
    w_uk_z = (sel[:, :, None, None] * wk[:, None]).reshape(N_HEADS, HEAD_GROUP * D_NOPE, D_C).astype(_BF16)
    wv = jnp.transpose(w_uv[0], (1, 0, 2))
    sel2 = (jnp.arange(2)[None, :] == (jnp.arange(N_HEADS) % 2)[:, None]).astype(_F32)
    w_uv_z = (wv[:, :, None, :] * sel2[:, None, :, None]).reshape(N_HEADS, D_C, 2 * D_V).astype(_BF16)
    in_wts = (row(g_attn_norm[0]), w_in_ext, row(g_q_norm[0]), w_qn, w_qr, w_qs, row(g_kv_norm[0]), w_uk_z)
    w_in_h = jnp.concatenate([wi[:, :o2], _pad_lanes(wi[:, o2:o3], D_NOPE),
                              _pad_lanes(_swap_halves(wi[:, o2:o3]), D_NOPE), wi[:, o3:]], axis=1).astype(_BF16)
    w_qa = _pad_lanes(wq).reshape(D_CQ, N_HEADS * HEAD_PAD).astype(_BF16)
    w_qb = _pad_lanes(_swap_halves(wq[:, :, D_NOPE:]), D_NOPE).reshape(D_CQ, N_HEADS * HEAD_PAD).astype(_BF16)
    w_kn = _pad_lanes(w_uk[0]).reshape(D_C, N_HEADS * HEAD_PAD).astype(_BF16)
    in_wts_h = (row(g_attn_norm[0]), w_in_h, row(g_q_norm[0]), w_qa, w_qb, row(g_kv_norm[0]), w_kn)
    conv_wts = (jnp.pad(w_dw[0], ((0, 1), (0, 0))), row(b_dw[0]), row(g_conv_ln[0]), row(b_conv_ln[0]),
                row(g_conv_out[0]))
    g_ao = row(g_attn_out[0])
    wo = w_o[0].astype(_BF16)
    w_fdw = jnp.pad(w_ffn_dw[0], ((0, SUBLANES - FFN_CONV_W), (0, 0)))
    ffn_wts = (wo[:D_ATT], wo[D_ATT:], row(g_ffn_norm[0]), w_up[0].astype(_BF16), w_fdw, row(b_ffn_dw[0]),
               w_down[0].astype(_BF16), row(g_final))

    rows_p = batch * seq
    tm = min(512, seq)
    xp = x_prompt.reshape(rows_p, d_model)
    cos_p, sin_p = _rope_tables(jnp.arange(seq, dtype=jnp.int32), lead=D_NOPE)
    q_p, kt_p, v_p, kv_p, glu_p = _inproj_heads(xp, cos_p, sin_p, in_wts_h, batch=batch, tm=tm, tq=Q_TILE)
    conv_p = _conv_prompt(glu_p, conv_wts, batch=batch, seq=seq, tc=tm)
    attn_p = _attn_prompt(q_p, kt_p, v_p, w_uv_z, g_ao, batch=batch, seq=seq, tk=min(512, seq))
    hist_p = SUBLANES
    prev_p = jnp.zeros((batch, hist_p, 2 * D_FF), _F32)
    tiled = lambda v: v.reshape(rows_p // tm, tm, v.shape[-1])
    y_p, tail_g, tail_v = _out_ffn(tiled(xp), tiled(attn_p), tiled(conv_p), prev_p, ffn_wts, block=(1, tm), shift=1,
                                   tiles_per_seq=seq // tm)
    last = seq // tm - 1
    tail_p = jnp.concatenate([tail_g[last::seq // tm], tail_v[last::seq // tm]], axis=-1)
    y_prompt = y_p.reshape(batch, seq, d_model)
    new_kv_prompt = kv_p.reshape(1, batch, seq, D_QK)
    new_conv_prompt = glu_p.reshape(batch, seq, D_CONV)[None, :, seq - (CONV_W - 1):]
    new_ffn_prompt = tail_p[None, :, hist_p - (FFN_CONV_W - 1):]

    rows_s = nb * steps
    xs = jnp.transpose(x_sample, (1, 0, 2)).reshape(rows_s, d_model)
    pos_s = past_len + jnp.arange(rows_s, dtype=jnp.int32) // nb
    cos_s, sin_s = _rope_tables(pos_s)
    tq_s = min(Q_TILE, nb)
    q_s, kv_s, kvb_s, glu_s = _inproj(xs, cos_s, sin_s, in_wts, tm=rows_s, tq=tq_s)
    q_s = q_s.reshape(steps, nb // tq_s, N_HEADS, tq_s, D_QK_PAD)
    q_s = jnp.transpose(q_s, (1, 3, 2, 0, 4)).reshape(nb, N_HEADS * steps, D_QK_PAD)
    kvn_s = jnp.transpose(kvb_s.reshape(steps, nb, D_QK_PAD), (1, 0, 2))
    ext = jnp.concatenate([jnp.transpose(state_conv[0], (1, 0, 2)), glu_s.reshape(steps, nb, D_CONV)], axis=0)
    conv_s = _conv_sample(ext, conv_wts, steps=steps).reshape(rows_s, D_CONV)
    attn_s = _attn_sample(page_table, q_s, kvn_s, jnp.swapaxes(cache_kv_latent[0], 1, 2), w_uv_z, g_ao)
    attn_s = jnp.transpose(attn_s, (1, 0, 2)).reshape(rows_s, D_ATT)
    sb = min(SAMPLE_FFN_SEQS, nb)
    n_tiles = nb // sb
    hist_s = (FFN_CONV_W - 1) * sb
    prev_s = jnp.transpose(state_ffn_conv[0].reshape(n_tiles, sb, FFN_CONV_W - 1, 2 * D_FF), (0, 2, 1, 3))
    prev_s = prev_s.reshape(n_tiles, hist_s, 2 * D_FF)
    steps3 = lambda v: v.reshape(steps, nb, v.shape[-1])
    y_s, tail_g, tail_v = _out_ffn(steps3(xs), steps3(attn_s), steps3(conv_s), prev_s, ffn_wts, block=(steps, sb),
                                   shift=sb, tiles_per_seq=1)
    tail_s = jnp.concatenate([tail_g, tail_v], axis=-1)
    y_sample = jnp.transpose(y_s, (1, 0, 2))
    new_kv_sample = jnp.transpose(kv_s.reshape(steps, nb, D_QK), (1, 0, 2))[None]
    new_conv_sample = jnp.transpose(ext[steps:], (1, 0, 2))[None]
    tail_s = tail_s.reshape(n_tiles, FFN_CONV_W - 1, sb, 2 * D_FF)
    new_ffn_sample = jnp.transpose(tail_s, (0, 2, 1, 3)).reshape(1, nb, FFN_CONV_W - 1, 2 * D_FF)
    return (y_prompt, y_sample, new_kv_prompt, new_conv_prompt, new_ffn_prompt,
            new_kv_sample, new_conv_sample, new_ffn_sample)
```

```python
import functools

import jax
import jax.numpy as jnp
from jax import lax
from jax.experimental import pallas as pl
from jax.experimental.pallas import tpu as pltpu

N_HEADS = 8
D_NOPE = 64
D_ROPE = 32
D_V = 64
D_C = 256
D_CQ = 384
D_ATT = N_HEADS * D_V
D_CONV = 512
CONV_W = 31
D_FF = 2816
FFN_CONV_W = 3
ROPE_THETA = 10000.0
PAGE_SIZE = 128
EPS = 1e-6
ATTN_SCALE = (D_NOPE + D_ROPE) ** -0.5
EXP2_SCALE = ATTN_SCALE * 1.4426950408889634

LANE = 128
SUBLANES = 8
D_QK = D_C + D_ROPE
D_QK_PAD = D_C + LANE
Q_TILE = 128
HEAD_GROUP = 256 // D_NOPE
HEAD_PAD = LANE
FFN_CHUNKS = 1
FFN_CHUNK = D_FF // FFN_CHUNKS
PAGES_PER_STEP = 64
SAMPLE_FFN_SEQS = 32
VMEM_LIMIT = 56 * 1024 * 1024

_BF16 = jnp.bfloat16
_F32 = jnp.float32


def _rms(x, g):
    return x * lax.rsqrt(jnp.mean(x * x, axis=-1, keepdims=True) + EPS) * g


def _dot(a, b):
    return jnp.dot(a, b, preferred_element_type=_F32)


def _dot_nt(a, b):
    return lax.dot_general(a, b, (((1,), (1,)), ((), ())), preferred_element_type=_F32)


def _const_spec(shape):
    nd = len(shape)
    return pl.BlockSpec(shape, lambda *_: (0,) * nd, pipeline_mode=pl.Buffered(1))


def _inproj_kernel(x_ref, cos_ref, sin_ref, g_in_ref, w_in_ref, g_q_ref, w_qn_ref, w_qr_ref,
                   w_qs_ref, g_kv_ref, w_uk_ref, q_ref, kv_ref, kvb_ref, glu_ref, *, tq):
    tm = x_ref.shape[0]
    cos = cos_ref[...]
    sin = sin_ref[...]
    h = _rms(x_ref[...], g_in_ref[...]).astype(_BF16)
    proj = _dot(h, w_in_ref[...])
    o_kv, o_kr, o_ks, o_ga, o_gb = D_CQ, D_CQ + D_C, D_CQ + D_C + LANE, D_CQ + D_C + 2 * LANE, D_CQ + D_C + 2 * LANE + D_CONV
    cq = proj[:, :o_kv]
    ckv = proj[:, o_kv:o_kr]
    glu_ref[...] = proj[:, o_ga:o_gb] * jax.nn.sigmoid(proj[:, o_gb:])
    ckv_n = _rms(ckv, g_kv_ref[...])
    kr = proj[:, o_kr:o_ks] * cos + proj[:, o_ks:o_ga] * sin
    kv_ref[:, :D_C] = ckv_n
    kv_ref[:, D_C:] = kr[:, :D_ROPE]
    kvb_ref[:, :D_C] = ckv_n.astype(_BF16)
    kvb_ref[:, D_C:] = kr.astype(_BF16)
    cqn = _rms(cq, g_q_ref[...]).astype(_BF16)
    qn = _dot(cqn, w_qn_ref[...]).astype(_BF16)
    qr = _dot(cqn, w_qr_ref[...])
    qs = _dot(cqn, w_qs_ref[...])
    for hd in range(N_HEADS):
        grp = hd // HEAD_GROUP
        ql = _dot(qn[:, grp * 256:(grp + 1) * 256], w_uk_ref[hd]).astype(_BF16)
        qrh = (qr[:, hd * LANE:(hd + 1) * LANE] * cos + qs[:, hd * LANE:(hd + 1) * LANE] * sin).astype(_BF16)
        for s in range(tm // tq):
            q_ref[s, hd, :, :D_C] = ql[s * tq:(s + 1) * tq]
            q_ref[s, hd, :, D_C:] = qrh[s * tq:(s + 1) * tq]


def _inproj(x, cos_tab, sin_tab, wts, *, tm, tq):
    rows = x.shape[0]
    n_tab = cos_tab.shape[0] // tm
    g_in, w_in, g_q, w_qn, w_qr, w_qs, g_kv, w_uk = wts
    row_spec = lambda w: pl.BlockSpec((tm, w), lambda i: (i, 0))
    tab_spec = pl.BlockSpec((tm, LANE), lambda i: (i % n_tab, 0))
    return pl.pallas_call(
        functools.partial(_inproj_kernel, tq=tq),
        grid=(rows // tm,),
        in_specs=[row_spec(x.shape[1]), tab_spec, tab_spec] + [_const_spec(w.shape) for w in wts],
        out_specs=[pl.BlockSpec((tm // tq, N_HEADS, tq, D_QK_PAD), lambda i: (i, 0, 0, 0)),
                   row_spec(D_QK), row_spec(D_QK_PAD), row_spec(D_CONV)],
        out_shape=[jax.ShapeDtypeStruct((rows // tq, N_HEADS, tq, D_QK_PAD), _BF16),
                   jax.ShapeDtypeStruct((rows, D_QK), _F32),
                   jax.ShapeDtypeStruct((rows, D_QK_PAD), _BF16),
                   jax.ShapeDtypeStruct((rows, D_CONV), _F32)],
        compiler_params=pltpu.CompilerParams(dimension_semantics=("parallel",), vmem_limit_bytes=VMEM_LIMIT),
        name="inproj",
    )(x, cos_tab, sin_tab, *wts)


def _inproj_heads_kernel(x_ref, cos_ref, sin_ref, g_in_ref, w_in_ref, g_q_ref, w_qa_ref, w_qb_ref, g_kv_ref,
                         w_kn_ref, q_ref, kt_ref, v_ref, kv_ref, glu_ref, *, tq):
    tm = x_ref.shape[0]
    cos = cos_ref[...]
    sin = sin_ref[...]
    h = _rms(x_ref[...], g_in_ref[...]).astype(_BF16)
    proj = _dot(h, w_in_ref[...])
    o_kv, o_kr, o_ks, o_ga, o_gb = D_CQ, D_CQ + D_C, D_CQ + D_C + LANE, D_CQ + D_C + 2 * LANE, D_CQ + D_C + 2 * LANE + D_CONV
    glu_ref[...] = proj[:, o_ga:o_gb] * jax.nn.sigmoid(proj[:, o_gb:])
    ckv_n = _rms(proj[:, o_kv:o_kr], g_kv_ref[...])
    kr = proj[:, o_kr:o_ks] * cos + proj[:, o_ks:o_ga] * sin
    kv_ref[:, :D_C] = ckv_n
    kv_ref[:, D_C:] = kr[:, D_NOPE:D_NOPE + D_ROPE]
    ckv_b = ckv_n.astype(_BF16)
    v_ref[...] = ckv_b
    k_nope = _dot(ckv_b, w_kn_ref[...])
    cqn = _rms(proj[:, :o_kv], g_q_ref[...]).astype(_BF16)
    qa = _dot(cqn, w_qa_ref[...])
    qb = _dot(cqn, w_qb_ref[...])
    for hd in range(N_HEADS):
        blk = slice(hd * HEAD_PAD, (hd + 1) * HEAD_PAD)
        q_h = ((qa[:, blk] * cos + qb[:, blk] * sin) * EXP2_SCALE).astype(_BF16)
        for s in range(tm // tq):
            q_ref[s, hd] = q_h[s * tq:(s + 1) * tq]
        kt_ref[0, hd] = (k_nope[:, blk] + kr).T.astype(_BF16)


def _inproj_heads(x, cos_tab, sin_tab, wts, *, batch, tm, tq):
    rows = x.shape[0]
    seq = rows // batch
    n_tab = seq // tm
    row_spec = lambda w: pl.BlockSpec((tm, w), lambda i: (i, 0))
    tab_spec = pl.BlockSpec((tm, LANE), lambda i: (i % n_tab, 0))
    return pl.pallas_call(
        functools.partial(_inproj_heads_kernel, tq=tq),
        grid=(rows // tm,),
        in_specs=[row_spec(x.shape[1]), tab_spec, tab_spec] + [_const_spec(w.shape) for w in wts],
        out_specs=[pl.BlockSpec((tm // tq, N_HEADS, tq, HEAD_PAD), lambda i: (i, 0, 0, 0)),
                   pl.BlockSpec((1, N_HEADS, HEAD_PAD, tm), lambda i: (i // n_tab, 0, 0, i % n_tab)),
                   row_spec(D_C), row_spec(D_QK), row_spec(D_CONV)],
        out_shape=[jax.ShapeDtypeStruct((rows // tq, N_HEADS, tq, HEAD_PAD), _BF16),
                   jax.ShapeDtypeStruct((batch, N_HEADS, HEAD_PAD, seq), _BF16),
                   jax.ShapeDtypeStruct((rows, D_C), _BF16),
                   jax.ShapeDtypeStruct((rows, D_QK), _F32),
                   jax.ShapeDtypeStruct((rows, D_CONV), _F32)],
        compiler_params=pltpu.CompilerParams(dimension_semantics=("parallel",), vmem_limit_bytes=VMEM_LIMIT),
        name="inproj_heads",
    )(x, cos_tab, sin_tab, *wts)


CONV_HALO = 32
CONV_ROWS = 16
CONV_UNROLL = 8


def _conv_post(y, g_ln, b_ln, g_out):
    mu = jnp.mean(y, axis=-1, keepdims=True)
    yc = y - mu
    z = yc * lax.rsqrt(jnp.mean(yc * yc, axis=-1, keepdims=True) + EPS) * g_ln + b_ln
    return _rms(z * jax.nn.sigmoid(z), g_out).astype(_BF16)


def _conv_prompt_kernel(cur_ref, halo_ref, w_ref, b_ref, g_ln_ref, b_ln_ref, g_out_ref, o_ref, ext_ref, sh_ref,
                        wb_ref):
    tc = cur_ref.shape[1]
    first = pl.program_id(1) == 0
    ext_ref[:CONV_HALO] = jnp.where(first, 0.0, halo_ref[0])
    ext_ref[CONV_HALO:CONV_HALO + tc] = cur_ref[0]
    ext_ref[CONV_HALO + tc:] = jnp.zeros((SUBLANES, D_CONV), _F32)
    off = CONV_HALO - (CONV_W - 1)
    for r in range(SUBLANES):
        sh_ref[r] = ext_ref[r:r + sh_ref.shape[1], :]
    for k in range(CONV_W):
        wb_ref[k] = jnp.broadcast_to(w_ref[k:k + 1, :], (SUBLANES, D_CONV))

    def body(it, carry):
        for u in range(CONV_UNROLL):
            base = pl.multiple_of((it * CONV_UNROLL + u) * CONV_ROWS, CONV_ROWS)
            acc = jnp.broadcast_to(b_ref[...], (CONV_ROWS, D_CONV))
            for k in range(CONV_W):
                r = (off + k) % SUBLANES
                w_tap = jnp.concatenate([wb_ref[k]] * (CONV_ROWS // SUBLANES), axis=0)
                acc = acc + w_tap * sh_ref[r, pl.ds(base + (off + k - r), CONV_ROWS), :]
            o_ref[pl.ds(base, CONV_ROWS), :] = _conv_post(acc, g_ln_ref[...], b_ln_ref[...], g_out_ref[...])
        return carry

    lax.fori_loop(0, tc // (CONV_ROWS * CONV_UNROLL), body, 0)


def _conv_prompt(glu, wts, *, batch, seq, tc):
    glu3 = glu.reshape(batch, seq, D_CONV)
    hb = tc // CONV_HALO
    return pl.pallas_call(
        _conv_prompt_kernel,
        grid=(batch, seq // tc),
        in_specs=[pl.BlockSpec((1, tc, D_CONV), lambda b, i: (b, i, 0)),
                  pl.BlockSpec((1, CONV_HALO, D_CONV), lambda b, i: (b, jnp.maximum(i * hb - 1, 0), 0))]
                 + [_const_spec(w.shape) for w in wts],
        out_specs=pl.BlockSpec((tc, D_CONV), lambda b, i: (b * (seq // tc) + i, 0)),
        out_shape=jax.ShapeDtypeStruct((batch * seq, D_CONV), _BF16),
        scratch_shapes=[pltpu.VMEM((CONV_HALO + tc + SUBLANES, D_CONV), _F32),
                        pltpu.VMEM((SUBLANES, CONV_HALO + tc, D_CONV), _F32),
                        pltpu.VMEM((CONV_W, SUBLANES, D_CONV), _F32)],
        compiler_params=pltpu.CompilerParams(dimension_semantics=("parallel", "parallel"), vmem_limit_bytes=VMEM_LIMIT),
        name="conv_prompt",
    )(glu3, glu3, *wts)


def _conv_sample_kernel(ext_ref, w_ref, b_ref, g_ln_ref, b_ln_ref, g_out_ref, o_ref):
    steps, nb = o_ref.shape[0], o_ref.shape[1]
    chunks = nb // CONV_ROWS

    def body(it, carry):
        t = it // chunks
        base = pl.multiple_of((it % chunks) * CONV_ROWS, CONV_ROWS)
        acc = jnp.broadcast_to(b_ref[...], (CONV_ROWS, D_CONV))
        for k in range(CONV_W):
            acc = acc + w_ref[k:k + 1, :] * ext_ref[t + k, pl.ds(base, CONV_ROWS), :]
        o_ref[t, pl.ds(base, CONV_ROWS), :] = _conv_post(acc, g_ln_ref[...], b_ln_ref[...], g_out_ref[...])
        return carry

    lax.fori_loop(0, steps * chunks, body, 0)


def _conv_sample(ext, wts, *, steps):
    nb = ext.shape[1]
    return pl.pallas_call(
        _conv_sample_kernel,
        out_shape=jax.ShapeDtypeStruct((steps, nb, D_CONV), _BF16),
        compiler_params=pltpu.CompilerParams(vmem_limit_bytes=VMEM_LIMIT),
        name="conv_sample",
    )(ext, *wts)


def _softmax_step(s, m_prev, l_prev, acc_prev, v):
    m_new = jnp.maximum(m_prev, jnp.max(s, axis=-1, keepdims=True))
    alpha = jnp.exp(m_prev - m_new)
    p = jnp.exp(s - m_new)
    l_new = alpha * l_prev + jnp.sum(p, axis=-1, keepdims=True)
    acc_new = alpha * acc_prev + _dot(p.astype(_BF16), v)
    return m_new, l_new, acc_new


def _attn_out(o, rows_per_head, w_uv_ref, g_ref):
    parts = []
    for pr in range(N_HEADS // 2):
        acc = None
        for hd in (2 * pr, 2 * pr + 1):
            oh = o[hd * rows_per_head:(hd + 1) * rows_per_head].astype(_BF16)
            term = _dot(oh, w_uv_ref[hd])
            acc = term if acc is None else acc + term
        parts.append(acc)
    attn = jnp.concatenate(parts, axis=-1)
    return _rms(attn, g_ref[...]).astype(_BF16)


def _attn_prompt_kernel(q_ref, kt_ref, v_ref, w_uv_ref, g_ref, o_ref, s_ref, s_next_ref, m_ref, l_ref, acc_ref,
                        *, tk):
    tq = q_ref.shape[2]
    rows = N_HEADS * tq
    qi = pl.program_id(1)
    m_ref[...] = jnp.full(m_ref.shape, -jnp.inf, _F32)
    l_ref[...] = jnp.zeros(l_ref.shape, _F32)
    acc_ref[...] = jnp.zeros(acc_ref.shape, _F32)
    n_full = (qi * tq) // tk

    def values(c):
        return v_ref[0, pl.ds(pl.multiple_of(c * tk, tk), tk), :]

    def scores_into(dst_ref, c):
        start = pl.multiple_of(c * tk, tk)
        for hd in range(N_HEADS):
            dst_ref[hd * tq:(hd + 1) * tq, :] = _dot(q_ref[0, hd], kt_ref[0, hd, :, pl.ds(start, tk)])

    def softmax_pv(s, v):
        width = s.shape[1]
        m_prev = m_ref[...]
        m_new = jnp.maximum(m_prev, jnp.max(s, axis=-1, keepdims=True))
        p = jnp.exp2(s - jnp.tile(m_new, (1, width // LANE)))
        alpha = jnp.exp2(m_prev - m_new)
        m_ref[...] = m_new
        l_ref[...] = alpha * l_ref[...] + jnp.sum(p, axis=-1, keepdims=True)
        acc_ref[...] = jnp.tile(alpha, (1, D_C // LANE)) * acc_ref[...] + _dot(p.astype(_BF16), v)

    scores_into(s_ref, 0)

    def pair(i, carry):
        c = 2 * i
        scores_into(s_next_ref, c + 1)
        softmax_pv(s_ref[...], values(c))
        scores_into(s_ref, c + 2)
        softmax_pv(s_next_ref[...], values(c + 1))
        return carry

    lax.fori_loop(0, n_full // 2, pair, 0)

    @pl.when(n_full % 2 == 1)
    def _():
        scores_into(s_next_ref, n_full)
        softmax_pv(s_ref[...], values(n_full - 1))
        s_ref[...] = s_next_ref[...]

    for r in range(tk // tq):
        @pl.when(qi % (tk // tq) == r)
        def _():
            width = (r + 1) * tq
            s = s_ref[:, :width]
            t_idx = r * tq + lax.broadcasted_iota(jnp.int32, (rows, width), 0) % tq
            k_idx = lax.broadcasted_iota(jnp.int32, (rows, width), 1)
            s = jnp.where(k_idx <= t_idx, s, -jnp.inf)
            softmax_pv(s, values(n_full)[:width])

    o_ref[...] = _attn_out(acc_ref[...] / jnp.tile(l_ref[...], (1, D_C // LANE)), tq, w_uv_ref, g_ref)


def _attn_prompt(q, kt, v, w_uv, g, *, batch, seq, tk):
    tq = q.shape[2]
    nq = seq // tq
    rows = N_HEADS * tq
    return pl.pallas_call(
        functools.partial(_attn_prompt_kernel, tk=tk),
        grid=(batch, nq),
        in_specs=[pl.BlockSpec((1, N_HEADS, tq, HEAD_PAD), lambda b, i: (b * nq + i, 0, 0, 0)),
                  pl.BlockSpec((1, N_HEADS, HEAD_PAD, seq), lambda b, i: (b, 0, 0, 0)),
                  pl.BlockSpec((1, seq, D_C), lambda b, i: (b, 0, 0)),
                  _const_spec(w_uv.shape), _const_spec(g.shape)],
        out_specs=pl.BlockSpec((tq, D_ATT), lambda b, i: (b * nq + i, 0)),
        out_shape=jax.ShapeDtypeStruct((batch * seq, D_ATT), _BF16),
        scratch_shapes=[pltpu.VMEM((rows, tk), _F32), pltpu.VMEM((rows, tk), _F32),
                        pltpu.VMEM((rows, LANE), _F32), pltpu.VMEM((rows, LANE), _F32),
                        pltpu.VMEM((rows, D_C), _F32)],
        compiler_params=pltpu.CompilerParams(dimension_semantics=("parallel", "parallel"), vmem_limit_bytes=VMEM_LIMIT),
        name="attn_prompt",
    )(q, kt, v.reshape(batch, seq, D_C), w_uv, g)


def _lanes_from_rows(col):
    rows = col.shape[0]
    sq = jnp.concatenate([jnp.broadcast_to(col, (rows, LANE)), jnp.zeros((LANE - rows, LANE), _F32)], axis=0)
    return sq.T[:, :rows]


def _attn_sample_kernel(pt_ref, q_ref, kvn_ref, w_uv_ref, g_ref, cache_ref, o_ref, pages_ref, sem_ref, kt_ref,
                        kn_ref, m_ref, l_ref, acct_ref, *, steps, n_pages):
    b = pl.program_id(0)
    rows = q_ref.shape[1]
    n_chunks = n_pages // PAGES_PER_STEP

    def page_copy(seq, chunk, p, slot):
        page = pt_ref[seq * n_pages + chunk * PAGES_PER_STEP + p]
        return pltpu.make_async_copy(cache_ref.at[page], pages_ref.at[slot, p], sem_ref.at[slot])

    def start_chunk(seq, chunk, slot):
        for p in range(PAGES_PER_STEP):
            page_copy(seq, chunk, p, slot).start()

    def wait_chunk(seq, chunk, slot):
        for p in range(PAGES_PER_STEP):
            page_copy(seq, chunk, p, slot).wait()

    @pl.when(b == 0)
    def _():
        start_chunk(0, 0, 0)

    kt_ref[D_QK:, :] = jnp.zeros((D_QK_PAD - D_QK, kt_ref.shape[1]), _BF16)
    m_ref[...] = jnp.full(m_ref.shape, -jnp.inf, _F32)
    l_ref[...] = jnp.zeros(l_ref.shape, _F32)
    acct_ref[...] = jnp.zeros(acct_ref.shape, _F32)

    for c in range(n_chunks):
        slot = c % 2
        if c + 1 < n_chunks:
            start_chunk(b, c + 1, 1 - slot)
        else:
            @pl.when(b + 1 < pl.num_programs(0))
            def _():
                start_chunk(b + 1, 0, 1 - slot)
        wait_chunk(b, c, slot)
        for p in range(PAGES_PER_STEP):
            kt_ref[:D_QK, p * PAGE_SIZE:(p + 1) * PAGE_SIZE] = pages_ref[slot, p].astype(_BF16)
        s = _dot(q_ref[0], kt_ref[...]) * ATTN_SCALE
        m_prev = m_ref[...]
        m_new = jnp.maximum(m_prev, jnp.max(s, axis=-1, keepdims=True))
        alpha = jnp.exp(m_prev - m_new)
        p_un = jnp.exp(s - m_new)
        m_ref[...] = m_new
        l_ref[...] = alpha * l_ref[...] + jnp.sum(p_un, axis=-1, keepdims=True)
        alpha_t = _lanes_from_rows(alpha)
        acct_ref[...] = (jnp.concatenate([alpha_t] * (D_C // LANE), axis=0) * acct_ref[...]
                         + _dot_nt(kt_ref[:D_C, :], p_un.astype(_BF16)))

    kn_ref[...] = jnp.zeros(kn_ref.shape, _BF16)
    kn_ref[:steps, :] = kvn_ref[0]
    kn = kn_ref[...]
    sn = _dot_nt(q_ref[0], kn) * ATTN_SCALE
    t_pos = lax.broadcasted_iota(jnp.int32, (rows, PAGE_SIZE), 0) % steps
    j_pos = lax.broadcasted_iota(jnp.int32, (rows, PAGE_SIZE), 1)
    sn = jnp.where(j_pos <= t_pos, sn, -jnp.inf)
    acc = jnp.concatenate([acct_ref[...], jnp.zeros((D_C, LANE - rows), _F32)], axis=1).T[:rows]
    _, l2, acc2 = _softmax_step(sn, m_ref[...], l_ref[...], acc, kn[:, :D_C])
    o_ref[0] = _attn_out(acc2 / l2, steps, w_uv_ref, g_ref)


def _attn_sample(page_table, q, kv_new, cache_t, w_uv, g):
    nb, rows, _ = q.shape
    steps = kv_new.shape[1]
    n_pages = page_table.shape[1]
    const = lambda shape: pl.BlockSpec(shape, lambda b, pt: (0,) * len(shape), pipeline_mode=pl.Buffered(1))
    return pl.pallas_call(
        functools.partial(_attn_sample_kernel, steps=steps, n_pages=n_pages),
        grid_spec=pltpu.PrefetchScalarGridSpec(
            num_scalar_prefetch=1,
            grid=(nb,),
            in_specs=[pl.BlockSpec((1, rows, D_QK_PAD), lambda b, pt: (b, 0, 0)),
                      pl.BlockSpec((1, steps, D_QK_PAD), lambda b, pt: (b, 0, 0)),
                      const(w_uv.shape), const(g.shape), pl.BlockSpec(memory_space=pl.ANY)],
            out_specs=pl.BlockSpec((1, steps, D_ATT), lambda b, pt: (b, 0, 0)),
            scratch_shapes=[pltpu.VMEM((2, PAGES_PER_STEP, D_QK, PAGE_SIZE), _F32),
                            pltpu.SemaphoreType.DMA((2,)),
                            pltpu.VMEM((D_QK_PAD, PAGES_PER_STEP * PAGE_SIZE), _BF16),
                            pltpu.VMEM((PAGE_SIZE, D_QK_PAD), _BF16),
                            pltpu.VMEM((rows, 1), _F32), pltpu.VMEM((rows, 1), _F32),
                            pltpu.VMEM((D_C, rows), _F32)]),
        out_shape=jax.ShapeDtypeStruct((nb, steps, D_ATT), _BF16),
        compiler_params=pltpu.CompilerParams(dimension_semantics=("arbitrary",), vmem_limit_bytes=VMEM_LIMIT),
        name="attn_sample",
    )(page_table.reshape(-1), q, kv_new, w_uv, g, cache_t)


def _out_ffn_kernel(x_ref, attn_ref, conv_ref, prev_g_ref, prev_v_ref, w_oa_ref, w_oc_ref, g_ffn_ref,
                    w_up_g_ref, w_up_v_ref, w_dw_g_ref, w_dw_v_ref, b_dw_g_ref, b_dw_v_ref, w_down_ref, g_fin_ref,
                    y_ref, tail_g_ref, tail_v_ref, ubuf_ref, carry_ref, acc_ref, h2_ref, *, shift, tiles_per_seq):
    gb, sb, d_model = x_ref.shape
    tm = gb * sb
    hist = prev_g_ref.shape[1]
    j = pl.program_id(1)
    n_chunks = pl.num_programs(1)
    first = pl.program_id(0) % tiles_per_seq == 0

    @pl.when(j == 0)
    def _():
        flat = lambda ref: ref[...].reshape(tm, ref.shape[2])
        acc_ref[...] = flat(x_ref) + _dot(flat(attn_ref), w_oa_ref[...]) + _dot(flat(conv_ref), w_oc_ref[...])
        h2_ref[...] = _rms(acc_ref[...], g_ffn_ref[...]).astype(_BF16)

    halves = []
    for part, (prev_ref, w_up_ref, w_dw_ref, b_dw_ref, tail_ref) in enumerate((
            (prev_g_ref, w_up_g_ref, w_dw_g_ref, b_dw_g_ref, tail_g_ref),
            (prev_v_ref, w_up_v_ref, w_dw_v_ref, b_dw_v_ref, tail_v_ref))):
        slot = part * n_chunks + j

        @pl.when(first)
        def _():
            ubuf_ref[part, :hist] = prev_ref[0]

        @pl.when(jnp.logical_not(first))
        def _():
            ubuf_ref[part, :hist] = carry_ref[slot]

        ubuf_ref[part, hist:] = _dot(h2_ref[...], w_up_ref[...])
        last_rows = ubuf_ref[part, tm:]
        carry_ref[slot] = last_rows
        tail_ref[0] = last_rows
        halves.append(b_dw_ref[...]
                      + w_dw_ref[0:1] * ubuf_ref[part, hist - 2 * shift:hist - 2 * shift + tm]
                      + w_dw_ref[1:2] * ubuf_ref[part, hist - shift:hist - shift + tm]
                      + w_dw_ref[2:3] * ubuf_ref[part, hist:])
    act = (halves[0] * jax.nn.sigmoid(halves[0]) * halves[1]).astype(_BF16)
    acc_ref[...] += _dot(act, w_down_ref[...])

    @pl.when(j == n_chunks - 1)
    def _():
        y_ref[...] = _rms(acc_ref[...], g_fin_ref[...]).reshape(gb, sb, d_model)


def _out_ffn(x, attn_n, conv_n, prev, wts, *, block, shift, tiles_per_seq):
    g_dim, s_dim, d_model = x.shape
    gb, sb = block
    n_s = s_dim // sb
    n_seq, hist, _ = prev.shape
    w_oa, w_oc, g_ffn, w_up, w_dw, b_dw, w_down, g_fin = wts
    fc, nj = FFN_CHUNK, FFN_CHUNKS
    row_spec = lambda w: pl.BlockSpec((gb, sb, w), lambda i, j: (i // n_s, i % n_s, 0))
    const = lambda w: pl.BlockSpec(w.shape, lambda i, j: (0,) * w.ndim, pipeline_mode=pl.Buffered(1))
    hist_spec = lambda part: pl.BlockSpec((1, hist, fc), lambda i, j: (i // tiles_per_seq, 0, part * nj + j))
    cols_spec = lambda rows, part: pl.BlockSpec((rows, fc), lambda i, j: (0, part * nj + j))
    tail_spec = pl.BlockSpec((1, hist, fc), lambda i, j: (i, 0, j))
    n_tiles = (g_dim // gb) * n_s
    return pl.pallas_call(
        functools.partial(_out_ffn_kernel, shift=shift, tiles_per_seq=tiles_per_seq),
        grid=(n_tiles, nj),
        in_specs=[row_spec(d_model), row_spec(D_ATT), row_spec(D_CONV), hist_spec(0), hist_spec(1),
                  const(w_oa), const(w_oc), const(g_ffn),
                  cols_spec(d_model, 0), cols_spec(d_model, 1), cols_spec(SUBLANES, 0), cols_spec(SUBLANES, 1),
                  cols_spec(1, 0), cols_spec(1, 1), pl.BlockSpec((fc, d_model), lambda i, j: (j, 0)), const(g_fin)],
        out_specs=[row_spec(d_model), tail_spec, tail_spec],
        out_shape=[jax.ShapeDtypeStruct(x.shape, _F32)] + [jax.ShapeDtypeStruct((n_tiles, hist, D_FF), _F32)] * 2,
        scratch_shapes=[pltpu.VMEM((2, hist + gb * sb, fc), _F32), pltpu.VMEM((2 * nj, hist, fc), _F32),
                        pltpu.VMEM((gb * sb, d_model), _F32), pltpu.VMEM((gb * sb, d_model), _BF16)],
        compiler_params=pltpu.CompilerParams(dimension_semantics=("arbitrary", "arbitrary"),
                                             vmem_limit_bytes=VMEM_LIMIT),
        name="out_ffn",
    )(x, attn_n, conv_n, prev, prev, w_oa, w_oc, g_ffn, w_up, w_up, w_dw, w_dw, b_dw, b_dw, w_down, g_fin)


def _rope_tables(pos, lead=0):
    half = D_ROPE // 2
    inv = ROPE_THETA ** (-(jnp.arange(half, dtype=_F32) * 2.0 / D_ROPE))
    ang = pos.astype(_F32)[:, None] * inv[None, :]
    cos, sin = jnp.cos(ang), jnp.sin(ang)
    pad = ((0, 0), (0, LANE - D_ROPE - lead))
    one, zero = jnp.ones((pos.shape[0], lead), _F32), jnp.zeros((pos.shape[0], lead), _F32)
    return (jnp.pad(jnp.concatenate([one, cos, cos], axis=-1), pad),
            jnp.pad(jnp.concatenate([zero, -sin, sin], axis=-1), pad))


def _swap_halves(w):
    half = D_ROPE // 2
    return jnp.concatenate([w[..., half:], w[..., :half]], axis=-1)


def _pad_lanes(w, lead=0):
    return jnp.pad(w, [(0, 0)] * (w.ndim - 1) + [(lead, LANE - w.shape[-1] - lead)])


def kernel(x_prompt, x_sample, cache_kv_latent, state_conv, state_ffn_conv, page_table, g_attn_norm, w_in, g_q_norm, w_uq, g_kv_norm, w_uk, w_uv, w_dw, b_dw, g_conv_ln, b_conv_ln, g_attn_out, g_conv_out, w_o, g_ffn_norm, w_up, w_ffn_dw, b_ffn_dw, w_down, g_final):
    batch, seq, d_model = x_prompt.shape
    nb, steps, _ = x_sample.shape
    n_pages = page_table.shape[1]
    past_len = n_pages * PAGE_SIZE
    assert w_in.shape[0] == 1, "single-layer trunk"
    assert seq % Q_TILE == 0 and nb % CONV_ROWS == 0 and steps <= PAGE_SIZE
    assert n_pages % (2 * PAGES_PER_STEP) == 0, "the two-slot page buffer needs an even number of chunks per sequence"

    row = lambda v: v.reshape(1, -1)
    o1, o2, o3 = D_CQ, D_CQ + D_C, D_CQ + D_C + D_ROPE
    wi = w_in[0]
    w_in_ext = jnp.concatenate([wi[:, :o2], _pad_lanes(wi[:, o2:o3]), _pad_lanes(_swap_halves(wi[:, o2:o3])),
                                wi[:, o3:]], axis=1).astype(_BF16)
    wq = w_uq[0]
    w_qn = wq[:, :, :D_NOPE].reshape(D_CQ, N_HEADS * D_NOPE).astype(_BF16)
    w_qr = _pad_lanes(wq[:, :, D_NOPE:]).reshape(D_CQ, N_HEADS * LANE).astype(_BF16)
    w_qs = _pad_lanes(_swap_halves(wq[:, :, D_NOPE:])).reshape(D_CQ, N_HEADS * LANE).astype(_BF16)
    wk = jnp.transpose(w_uk[0], (1, 2, 0))
    sel = (jnp.arange(HEAD_GROUP)[None, :] == (jnp.arange(N_HEADS) % HEAD_GROUP)[:, None]).astype(_F32)
    w_uk_z = (sel[:, :, None, None] * wk[:, None]).reshape(N_HEADS, HEAD_GROUP * D_NOPE, D_C).astype(_BF16)
    wv = jnp.transpose(w_uv[0], (1, 0, 2))
    sel2 = (jnp.arange(2)[None, :] == (jnp.arange(N_HEADS) % 2)[:, None]).astype(_F32)
    w_uv_z = (wv[:, :, None, :] * sel2[:, None, :, None]).reshape(N_HEADS, D_C, 2 * D_V).astype(_BF16)
    in_wts = (row(g_attn_norm[0]), w_in_ext, row(g_q_norm[0]), w_qn, w_qr, w_qs, row(g_kv_norm[0]), w_uk_z)
    w_in_h = jnp.concatenate([wi[:, :o2], _pad_lanes(wi[:, o2:o3], D_NOPE),
                              _pad_lanes(_swap_halves(wi[:, o2:o3]), D_NOPE), wi[:, o3:]], axis=1).astype(_BF16)
    w_qa = _pad_lanes(wq).reshape(D_CQ, N_HEADS * HEAD_PAD).astype(_BF16)
    w_qb = _pad_lanes(_swap_halves(wq[:, :, D_NOPE:]), D_NOPE).reshape(D_CQ, N_HEADS * HEAD_PAD).astype(_BF16)
    w_kn = _pad_lanes(w_uk[0]).reshape(D_C, N_HEADS * HEAD_PAD).astype(_BF16)
    in_wts_h = (row(g_attn_norm[0]), w_in_h, row(g_q_norm[0]), w_qa, w_qb, row(g_kv_norm[0]), w_kn)
    conv_wts = (jnp.pad(w_dw[0], ((0, 1), (0, 0))), row(b_dw[0]), row(g_conv_ln[0]), row(b_conv_ln[0]),
                row(g_conv_out[0]))
    g_ao = row(g_attn_out[0])
    wo = w_o[0].astype(_BF16)
    w_fdw = jnp.pad(w_ffn_dw[0], ((0, SUBLANES - FFN_CONV_W), (0, 0)))
    ffn_wts = (wo[:D_ATT], wo[D_ATT:], row(g_ffn_norm[0]), w_up[0].astype(_BF16), w_fdw, row(b_ffn_dw[0]),
               w_down[0].astype(_BF16), row(g_final))

    rows_p = batch * seq
    tm = min(512, seq)
    xp = x_prompt.reshape(rows_p, d_model)
    cos_p, sin_p = _rope_tables(jnp.arange(seq, dtype=jnp.int32), lead=D_NOPE)
    q_p, kt_p, v_p, kv_p, glu_p = _inproj_heads(xp, cos_p, sin_p, in_wts_h, batch=batch, tm=tm, tq=Q_TILE)
    conv_p = _conv_prompt(glu_p, conv_wts, batch=batch, seq=seq, tc=tm)
    attn_p = _attn_prompt(q_p, kt_p, v_p, w_uv_z, g_ao, batch=batch, seq=seq, tk=min(512, seq))
    hist_p = SUBLANES
    prev_p = jnp.zeros((batch, hist_p, 2 * D_FF), _F32)
    tiled = lambda v: v.reshape(rows_p // tm, tm, v.shape[-1])
    y_p, tail_g, tail_v = _out_ffn(tiled(xp), tiled(attn_p), tiled(conv_p), prev_p, ffn_wts, block=(1, tm), shift=1,
                                   tiles_per_seq=seq // tm)
    last = seq // tm - 1
    tail_p = jnp.concatenate([tail_g[last::seq // tm], tail_v[last::seq // tm]], axis=-1)
    y_prompt = y_p.reshape(batch, seq, d_model)
    new_kv_prompt = kv_p.reshape(1, batch, seq, D_QK)
    new_conv_prompt = glu_p.reshape(batch, seq, D_CONV)[None, :, seq - (CONV_W - 1):]
    new_ffn_prompt = tail_p[None, :, hist_p - (FFN_CONV_W - 1):]

    rows_s = nb * steps
    xs = jnp.transpose(x_sample, (1, 0, 2)).reshape(rows_s, d_model)
    pos_s = past_len + jnp.arange(rows_s, dtype=jnp.int32) // nb
    cos_s, sin_s = _rope_tables(pos_s)
    tq_s = min(Q_TILE, nb)
    q_s, kv_s, kvb_s, glu_s = _inproj(xs, cos_s, sin_s, in_wts, tm=rows_s, tq=tq_s)
    q_s = q_s.reshape(steps, nb // tq_s, N_HEADS, tq_s, D_QK_PAD)
    q_s = jnp.transpose(q_s, (1, 3, 2, 0, 4)).reshape(nb, N_HEADS * steps, D_QK_PAD)
    kvn_s = jnp.transpose(kvb_s.reshape(steps, nb, D_QK_PAD), (1, 0, 2))
    ext = jnp.concatenate([jnp.transpose(state_conv[0], (1, 0, 2)), glu_s.reshape(steps, nb, D_CONV)], axis=0)
    conv_s = _conv_sample(ext, conv_wts, steps=steps).reshape(rows_s, D_CONV)
    attn_s = _attn_sample(page_table, q_s, kvn_s, jnp.swapaxes(cache_kv_latent[0], 1, 2), w_uv_z, g_ao)
    attn_s = jnp.transpose(attn_s, (1, 0, 2)).reshape(rows_s, D_ATT)
    sb = min(SAMPLE_FFN_SEQS, nb)
    n_tiles = nb // sb
    hist_s = (FFN_CONV_W - 1) * sb
    prev_s = jnp.transpose(state_ffn_conv[0].reshape(n_tiles, sb, FFN_CONV_W - 1, 2 * D_FF), (0, 2, 1, 3))
    prev_s = prev_s.reshape(n_tiles, hist_s, 2 * D_FF)
    steps3 = lambda v: v.reshape(steps, nb, v.shape[-1])
    y_s, tail_g, tail_v = _out_ffn(steps3(xs), steps3(attn_s), steps3(conv_s), prev_s, ffn_wts, block=(steps, sb),
                                   shift=sb, tiles_per_seq=1)
    tail_s = jnp.concatenate([tail_g, tail_v], axis=-1)
    y_sample = jnp.transpose(y_s, (1, 0, 2))
    new_kv_sample = jnp.transpose(kv_s.reshape(steps, nb, D_QK), (1, 0, 2))[None]
    new_conv_sample = jnp.transpose(ext[steps:], (1, 0, 2))[None]
    tail_s = tail_s.reshape(n_tiles, FFN_CONV_W - 1, sb, 2 * D_FF)
    new_ffn_sample = jnp.transpose(tail_s, (0, 2, 1, 3)).reshape(1, nb, FFN_CONV_W - 1, 2 * D_FF)
    return (y_prompt, y_sample, new_kv_prompt, new_conv_prompt, new_ffn_prompt,
            new_kv_sample, new_conv_sample, new_ffn_sample)
```

```python
import functools
import math

import jax
import jax.numpy as jnp
from jax import lax
from jax.experimental import pallas as pl
from jax.experimental.pallas import tpu as pltpu

N_HEADS = 8
D_NOPE = 64
D_ROPE = 32
D_V = 64
D_C = 256
D_CQ = 384
D_ATT = N_HEADS * D_V
D_CONV = 512
CONV_W = 31
D_FF = 2816
FFN_CONV_W = 3
ROPE_THETA = 10000.0
PAGE_SIZE = 128
EPS = 1e-6
ATTN_SCALE = (D_NOPE + D_ROPE) ** -0.5
EXP2_SCALE = ATTN_SCALE * 1.4426950408889634

LANE = 128
SUBLANES = 8
D_QK = D_C + D_ROPE
D_QK_PAD = D_C + LANE
Q_TILE = 128
HEAD_GROUP = 256 // D_NOPE
HEAD_PAD = LANE
FFN_CHUNKS = 1
FFN_CHUNK = D_FF // FFN_CHUNKS
PAGES_PER_STEP = 64
SAMPLE_FFN_SEQS = 32
VMEM_LIMIT = 56 * 1024 * 1024

_BF16 = jnp.bfloat16
_F32 = jnp.float32


def _rms(x, g):
    return x * lax.rsqrt(jnp.mean(x * x, axis=-1, keepdims=True) + EPS) * g


def _dot(a, b):
    return jnp.dot(a, b, preferred_element_type=_F32)


def _dot_nt(a, b):
    return lax.dot_general(a, b, (((1,), (1,)), ((), ())), preferred_element_type=_F32)


def _const_spec(shape):
    nd = len(shape)
    return pl.BlockSpec(shape, lambda *_: (0,) * nd, pipeline_mode=pl.Buffered(1))


def _inproj_kernel(x_ref, cos_ref, sin_ref, g_in_ref, w_in_ref, g_q_ref, w_qn_ref, w_qr_ref,
                   w_qs_ref, g_kv_ref, w_uk_ref, q_ref, kv_ref, kvb_ref, glu_ref, *, tq):
    tm = x_ref.shape[0]
    cos = cos_ref[...]
    sin = sin_ref[...]
    h = _rms(x_ref[...], g_in_ref[...]).astype(_BF16)
    proj = _dot(h, w_in_ref[...])
    o_kv, o_kr, o_ks, o_ga, o_gb = D_CQ, D_CQ + D_C, D_CQ + D_C + LANE, D_CQ + D_C + 2 * LANE, D_CQ + D_C + 2 * LANE + D_CONV
    cq = proj[:, :o_kv]
    ckv = proj[:, o_kv:o_kr]
    glu_ref[...] = proj[:, o_ga:o_gb] * jax.nn.sigmoid(proj[:, o_gb:])
    ckv_n = _rms(ckv, g_kv_ref[...])
    kr = proj[:, o_kr:o_ks] * cos + proj[:, o_ks:o_ga] * sin
    kv_ref[:, :D_C] = ckv_n
    kv_ref[:, D_C:] = kr[:, :D_ROPE]
    kvb_ref[:, :D_C] = ckv_n.astype(_BF16)
    kvb_ref[:, D_C:] = kr.astype(_BF16)
    cqn = _rms(cq, g_q_ref[...]).astype(_BF16)
    qn = _dot(cqn, w_qn_ref[...]).astype(_BF16)
    qr = _dot(cqn, w_qr_ref[...])
    qs = _dot(cqn, w_qs_ref[...])
    for hd in range(N_HEADS):
        grp = hd // HEAD_GROUP
        ql = _dot(qn[:, grp * 256:(grp + 1) * 256], w_uk_ref[hd]).astype(_BF16)
        qrh = (qr[:, hd * LANE:(hd + 1) * LANE] * cos + qs[:, hd * LANE:(hd + 1) * LANE] * sin).astype(_BF16)
        for s in range(tm // tq):
            q_ref[s, hd, :, :D_C] = ql[s * tq:(s + 1) * tq]
            q_ref[s, hd, :, D_C:] = qrh[s * tq:(s + 1) * tq]


def _inproj(x, cos_tab, sin_tab, wts, *, tm, tq):
    rows = x.shape[0]
    n_tab = cos_tab.shape[0] // tm
    g_in, w_in, g_q, w_qn, w_qr, w_qs, g_kv, w_uk = wts
    row_spec = lambda w: pl.BlockSpec((tm, w), lambda i: (i, 0))
    tab_spec = pl.BlockSpec((tm, LANE), lambda i: (i % n_tab, 0))
    return pl.pallas_call(
        functools.partial(_inproj_kernel, tq=tq),
        grid=(rows // tm,),
        in_specs=[row_spec(x.shape[1]), tab_spec, tab_spec] + [_const_spec(w.shape) for w in wts],
        out_specs=[pl.BlockSpec((tm // tq, N_HEADS, tq, D_QK_PAD), lambda i: (i, 0, 0, 0)),
                   row_spec(D_QK), row_spec(D_QK_PAD), row_spec(D_CONV)],
        out_shape=[jax.ShapeDtypeStruct((rows // tq, N_HEADS, tq, D_QK_PAD), _BF16),
                   jax.ShapeDtypeStruct((rows, D_QK), _F32),
                   jax.ShapeDtypeStruct((rows, D_QK_PAD), _BF16),
                   jax.ShapeDtypeStruct((rows, D_CONV), _F32)],
        compiler_params=pltpu.CompilerParams(dimension_semantics=("parallel",), vmem_limit_bytes=VMEM_LIMIT),
        name="inproj",
    )(x, cos_tab, sin_tab, *wts)


def _inproj_heads_kernel(x_ref, cos_ref, sin_ref, g_in_ref, w_in_ref, g_q_ref, w_qa_ref, w_qb_ref, g_kv_ref,
                         w_kn_ref, q_ref, kt_ref, v_ref, kv_ref, glu_ref, *, tq):
    tm = x_ref.shape[0]
    cos = cos_ref[...]
    sin = sin_ref[...]
    h = _rms(x_ref[...], g_in_ref[...]).astype(_BF16)
    proj = _dot(h, w_in_ref[...])
    o_kv, o_kr, o_ks, o_ga, o_gb = D_CQ, D_CQ + D_C, D_CQ + D_C + LANE, D_CQ + D_C + 2 * LANE, D_CQ + D_C + 2 * LANE + D_CONV
    glu_ref[...] = proj[:, o_ga:o_gb] * jax.nn.sigmoid(proj[:, o_gb:])
    ckv_n = _rms(proj[:, o_kv:o_kr], g_kv_ref[...])
    kr = proj[:, o_kr:o_ks] * cos + proj[:, o_ks:o_ga] * sin
    kv_ref[:, :D_C] = ckv_n
    kv_ref[:, D_C:] = kr[:, D_NOPE:D_NOPE + D_ROPE]
    ckv_b = ckv_n.astype(_BF16)
    v_ref[...] = ckv_b
    k_nope = _dot(ckv_b, w_kn_ref[...])
    cqn = _rms(proj[:, :o_kv], g_q_ref[...]).astype(_BF16)
    qa = _dot(cqn, w_qa_ref[...])
    qb = _dot(cqn, w_qb_ref[...])
    for hd in range(N_HEADS):
        blk = slice(hd * HEAD_PAD, (hd + 1) * HEAD_PAD)
        q_h = ((qa[:, blk] * cos + qb[:, blk] * sin) * EXP2_SCALE).astype(_BF16)
        for s in range(tm // tq):
            q_ref[s, hd] = q_h[s * tq:(s + 1) * tq]
        kt_ref[0, hd] = (k_nope[:, blk] + kr).T.astype(_BF16)


def _inproj_heads(x, cos_tab, sin_tab, wts, *, batch, tm, tq):
    rows = x.shape[0]
    seq = rows // batch
    n_tab = seq // tm
    row_spec = lambda w: pl.BlockSpec((tm, w), lambda i: (i, 0))
    tab_spec = pl.BlockSpec((tm, LANE), lambda i: (i % n_tab, 0))
    return pl.pallas_call(
        functools.partial(_inproj_heads_kernel, tq=tq),
        grid=(rows // tm,),
        in_specs=[row_spec(x.shape[1]), tab_spec, tab_spec] + [_const_spec(w.shape) for w in wts],
        out_specs=[pl.BlockSpec((tm // tq, N_HEADS, tq, HEAD_PAD), lambda i: (i, 0, 0, 0)),
                   pl.BlockSpec((1, N_HEADS, HEAD_PAD, tm), lambda i: (i // n_tab, 0, 0, i % n_tab)),
                   row_spec(D_C), row_spec(D_QK), row_spec(D_CONV)],
        out_shape=[jax.ShapeDtypeStruct((rows // tq, N_HEADS, tq, HEAD_PAD), _BF16),
                   jax.ShapeDtypeStruct((batch, N_HEADS, HEAD_PAD, seq), _BF16),
                   jax.ShapeDtypeStruct((rows, D_C), _BF16),
                   jax.ShapeDtypeStruct((rows, D_QK), _F32),
                   jax.ShapeDtypeStruct((rows, D_CONV), _F32)],
        compiler_params=pltpu.CompilerParams(dimension_semantics=("parallel",), vmem_limit_bytes=VMEM_LIMIT),
        name="inproj_heads",
    )(x, cos_tab, sin_tab, *wts)


CONV_HALO = 32
CONV_ROWS = 16
CONV_UNROLL = 8


def _conv_post(y, g_ln, b_ln, g_out):
    mu = jnp.mean(y, axis=-1, keepdims=True)
    yc = y - mu
    z = yc * lax.rsqrt(jnp.mean(yc * yc, axis=-1, keepdims=True) + EPS) * g_ln + b_ln
    return _rms(z * jax.nn.sigmoid(z), g_out).astype(_BF16)


def _conv_prompt_kernel(cur_ref, halo_ref, w_ref, b_ref, g_ln_ref, b_ln_ref, g_out_ref, o_ref, ext_ref, sh_ref,
                        wb_ref):
    tc = cur_ref.shape[1]
    first = pl.program_id(1) == 0
    ext_ref[:CONV_HALO] = jnp.where(first, 0.0, halo_ref[0])
    ext_ref[CONV_HALO:CONV_HALO + tc] = cur_ref[0]
    ext_ref[CONV_HALO + tc:] = jnp.zeros((SUBLANES, D_CONV), _F32)
    off = CONV_HALO - (CONV_W - 1)
    for r in range(SUBLANES):
        sh_ref[r] = ext_ref[r:r + sh_ref.shape[1], :]
    for k in range(CONV_W):
        wb_ref[k] = jnp.broadcast_to(w_ref[k:k + 1, :], (SUBLANES, D_CONV))

    def body(it, carry):
        for u in range(CONV_UNROLL):
            base = pl.multiple_of((it * CONV_UNROLL + u) * CONV_ROWS, CONV_ROWS)
            acc = jnp.broadcast_to(b_ref[...], (CONV_ROWS, D_CONV))
            for k in range(CONV_W):
                r = (off + k) % SUBLANES
                w_tap = jnp.concatenate([wb_ref[k]] * (CONV_ROWS // SUBLANES), axis=0)
                acc = acc + w_tap * sh_ref[r, pl.ds(base + (off + k - r), CONV_ROWS), :]
            o_ref[pl.ds(base, CONV_ROWS), :] = _conv_post(acc, g_ln_ref[...], b_ln_ref[...], g_out_ref[...])
        return carry

    lax.fori_loop(0, tc // (CONV_ROWS * CONV_UNROLL), body, 0)


def _conv_prompt(glu, wts, *, batch, seq, tc):
    glu3 = glu.reshape(batch, seq, D_CONV)
    hb = tc // CONV_HALO
    return pl.pallas_call(
        _conv_prompt_kernel,
        grid=(batch, seq // tc),
        in_specs=[pl.BlockSpec((1, tc, D_CONV), lambda b, i: (b, i, 0)),
                  pl.BlockSpec((1, CONV_HALO, D_CONV), lambda b, i: (b, jnp.maximum(i * hb - 1, 0), 0))]
                 + [_const_spec(w.shape) for w in wts],
        out_specs=pl.BlockSpec((tc, D_CONV), lambda b, i: (b * (seq // tc) + i, 0)),
        out_shape=jax.ShapeDtypeStruct((batch * seq, D_CONV), _BF16),
        scratch_shapes=[pltpu.VMEM((CONV_HALO + tc + SUBLANES, D_CONV), _F32),
                        pltpu.VMEM((SUBLANES, CONV_HALO + tc, D_CONV), _F32),
                        pltpu.VMEM((CONV_W, SUBLANES, D_CONV), _F32)],
        compiler_params=pltpu.CompilerParams(dimension_semantics=("parallel", "parallel"), vmem_limit_bytes=VMEM_LIMIT),
        name="conv_prompt",
    )(glu3, glu3, *wts)


def _conv_sample_kernel(ext_ref, w_ref, b_ref, g_ln_ref, b_ln_ref, g_out_ref, o_ref):
    steps, nb = o_ref.shape[0], o_ref.shape[1]
    chunks = nb // CONV_ROWS
    unroll = math.gcd(CONV_UNROLL, chunks)

    def body(it, carry):
        t = it // (chunks // unroll)
        first = (it % (chunks // unroll)) * unroll
        for u in range(unroll):
            base = pl.multiple_of((first + u) * CONV_ROWS, CONV_ROWS)
            acc = jnp.broadcast_to(b_ref[...], (CONV_ROWS, D_CONV))
            for k in range(CONV_W):
                acc = acc + w_ref[k:k + 1, :] * ext_ref[t + k, pl.ds(base, CONV_ROWS), :]
            o_ref[t, pl.ds(base, CONV_ROWS), :] = _conv_post(acc, g_ln_ref[...], b_ln_ref[...], g_out_ref[...])
        return carry

    lax.fori_loop(0, steps * chunks // unroll, body, 0)


def _conv_sample(ext, wts, *, steps):
    nb = ext.shape[1]
    return pl.pallas_call(
        _conv_sample_kernel,
        out_shape=jax.ShapeDtypeStruct((steps, nb, D_CONV), _BF16),
        compiler_params=pltpu.CompilerParams(vmem_limit_bytes=VMEM_LIMIT),
        name="conv_sample",
    )(ext, *wts)


def _softmax_step(s, m_prev, l_prev, acc_prev, v):
    m_new = jnp.maximum(m_prev, jnp.max(s, axis=-1, keepdims=True))
    alpha = jnp.exp(m_prev - m_new)
    p = jnp.exp(s - m_new)
    l_new = alpha * l_prev + jnp.sum(p, axis=-1, keepdims=True)
    acc_new = alpha * acc_prev + _dot(p.astype(_BF16), v)
    return m_new, l_new, acc_new


def _attn_out(o, rows_per_head, w_uv_ref, g_ref):
    parts = []
    for pr in range(N_HEADS // 2):
        acc = None
        for hd in (2 * pr, 2 * pr + 1):
            oh = o[hd * rows_per_head:(hd + 1) * rows_per_head].astype(_BF16)
            term = _dot(oh, w_uv_ref[hd])
            acc = term if acc is None else acc + term
        parts.append(acc)
    attn = jnp.concatenate(parts, axis=-1)
    return _rms(attn, g_ref[...]).astype(_BF16)


def _attn_prompt_kernel(q_ref, kt_ref, v_ref, w_uv_ref, g_ref, o_ref, s_ref, s_next_ref, m_ref, l_ref, acc_ref,
                        *, tk):
    tq = q_ref.shape[2]
    rows = N_HEADS * tq
    qi = pl.program_id(1)
    m_ref[...] = jnp.full(m_ref.shape, -jnp.inf, _F32)
    l_ref[...] = jnp.zeros(l_ref.shape, _F32)
    acc_ref[...] = jnp.zeros(acc_ref.shape, _F32)
    n_full = (qi * tq) // tk

    def values(c):
        return v_ref[0, pl.ds(pl.multiple_of(c * tk, tk), tk), :]

    def scores_into(dst_ref, c):
        start = pl.multiple_of(c * tk, tk)
        for hd in range(N_HEADS):
            dst_ref[hd * tq:(hd + 1) * tq, :] = _dot(q_ref[0, hd], kt_ref[0, hd, :, pl.ds(start, tk)])

    def softmax_pv(s, v):
        width = s.shape[1]
        m_prev = m_ref[...]
        m_new = jnp.maximum(m_prev, jnp.max(s, axis=-1, keepdims=True))
        p = jnp.exp2(s - jnp.tile(m_new, (1, width // LANE)))
        alpha = jnp.exp2(m_prev - m_new)
        m_ref[...] = m_new
        l_ref[...] = alpha * l_ref[...] + jnp.sum(p, axis=-1, keepdims=True)
        acc_ref[...] = jnp.tile(alpha, (1, D_C // LANE)) * acc_ref[...] + _dot(p.astype(_BF16), v)

    scores_into(s_ref, 0)

    def pair(i, carry):
        c = 2 * i
        scores_into(s_next_ref, c + 1)
        softmax_pv(s_ref[...], values(c))
        scores_into(s_ref, c + 2)
        softmax_pv(s_next_ref[...], values(c + 1))
        return carry

    lax.fori_loop(0, n_full // 2, pair, 0)

    @pl.when(n_full % 2 == 1)
    def _():
        scores_into(s_next_ref, n_full)
        softmax_pv(s_ref[...], values(n_full - 1))
        s_ref[...] = s_next_ref[...]

    for r in range(tk // tq):
        @pl.when(qi % (tk // tq) == r)
        def _():
            width = (r + 1) * tq
            s = s_ref[:, :width]
            t_idx = r * tq + lax.broadcasted_iota(jnp.int32, (rows, width), 0) % tq
            k_idx = lax.broadcasted_iota(jnp.int32, (rows, width), 1)
            s = jnp.where(k_idx <= t_idx, s, -jnp.inf)
            softmax_pv(s, values(n_full)[:width])

    o_ref[...] = _attn_out(acc_ref[...] / jnp.tile(l_ref[...], (1, D_C // LANE)), tq, w_uv_ref, g_ref)


def _attn_prompt(q, kt, v, w_uv, g, *, batch, seq, tk):
    tq = q.shape[2]
    nq = seq // tq
    rows = N_HEADS * tq
    return pl.pallas_call(
        functools.partial(_attn_prompt_kernel, tk=tk),
        grid=(batch, nq),
        in_specs=[pl.BlockSpec((1, N_HEADS, tq, HEAD_PAD), lambda b, i: (b * nq + i, 0, 0, 0)),
                  pl.BlockSpec((1, N_HEADS, HEAD_PAD, seq), lambda b, i: (b, 0, 0, 0)),
                  pl.BlockSpec((1, seq, D_C), lambda b, i: (b, 0, 0)),
                  _const_spec(w_uv.shape), _const_spec(g.shape)],
        out_specs=pl.BlockSpec((tq, D_ATT), lambda b, i: (b * nq + i, 0)),
        out_shape=jax.ShapeDtypeStruct((batch * seq, D_ATT), _BF16),
        scratch_shapes=[pltpu.VMEM((rows, tk), _F32), pltpu.VMEM((rows, tk), _F32),
                        pltpu.VMEM((rows, LANE), _F32), pltpu.VMEM((rows, LANE), _F32),
                        pltpu.VMEM((rows, D_C), _F32)],
        compiler_params=pltpu.CompilerParams(dimension_semantics=("parallel", "parallel"), vmem_limit_bytes=VMEM_LIMIT),
        name="attn_prompt",
    )(q, kt, v.reshape(batch, seq, D_C), w_uv, g)


def _lanes_from_rows(col):
    rows = col.shape[0]
    sq = jnp.concatenate([jnp.broadcast_to(col, (rows, LANE)), jnp.zeros((LANE - rows, LANE), _F32)], axis=0)
    return sq.T[:, :rows]


def _attn_sample_kernel(pt_ref, q_ref, kvn_ref, w_uv_ref, g_ref, cache_ref, o_ref, pages_ref, sem_ref, kt_ref,
                        kn_ref, m_ref, l_ref, acct_ref, *, steps, n_pages):
    b = pl.program_id(0)
    rows = q_ref.shape[1]
    n_chunks = n_pages // PAGES_PER_STEP

    def page_copy(seq, chunk, p, slot):
        page = pt_ref[seq * n_pages + chunk * PAGES_PER_STEP + p]
        return pltpu.make_async_copy(cache_ref.at[page], pages_ref.at[slot, p], sem_ref.at[slot])

    def start_chunk(seq, chunk, slot):
        for p in range(PAGES_PER_STEP):
            page_copy(seq, chunk, p, slot).start()

    def wait_chunk(seq, chunk, slot):
        for p in range(PAGES_PER_STEP):
            page_copy(seq, chunk, p, slot).wait()

    @pl.when(b == 0)
    def _():
        start_chunk(0, 0, 0)

    kt_ref[D_QK:, :] = jnp.zeros((D_QK_PAD - D_QK, kt_ref.shape[1]), _BF16)
    m_ref[...] = jnp.full(m_ref.shape, -jnp.inf, _F32)
    l_ref[...] = jnp.zeros(l_ref.shape, _F32)
    acct_ref[...] = jnp.zeros(acct_ref.shape, _F32)

    for c in range(n_chunks):
        slot = c % 2
        if c + 1 < n_chunks:
            start_chunk(b, c + 1, 1 - slot)
        else:
            @pl.when(b + 1 < pl.num_programs(0))
            def _():
                start_chunk(b + 1, 0, 1 - slot)
        wait_chunk(b, c, slot)
        for p in range(PAGES_PER_STEP):
            kt_ref[:D_QK, p * PAGE_SIZE:(p + 1) * PAGE_SIZE] = pages_ref[slot, p].astype(_BF16)
        s = _dot(q_ref[0], kt_ref[...]) * ATTN_SCALE
        m_prev = m_ref[...]
        m_new = jnp.maximum(m_prev, jnp.max(s, axis=-1, keepdims=True))
        alpha = jnp.exp(m_prev - m_new)
        p_un = jnp.exp(s - m_new)
        m_ref[...] = m_new
        l_ref[...] = alpha * l_ref[...] + jnp.sum(p_un, axis=-1, keepdims=True)
        alpha_t = _lanes_from_rows(alpha)
        acct_ref[...] = (jnp.concatenate([alpha_t] * (D_C // LANE), axis=0) * acct_ref[...]
                         + _dot_nt(kt_ref[:D_C, :], p_un.astype(_BF16)))

    kn_ref[...] = jnp.zeros(kn_ref.shape, _BF16)
    kn_ref[:steps, :] = kvn_ref[0]
    kn = kn_ref[...]
    sn = _dot_nt(q_ref[0], kn) * ATTN_SCALE
    t_pos = lax.broadcasted_iota(jnp.int32, (rows, PAGE_SIZE), 0) % steps
    j_pos = lax.broadcasted_iota(jnp.int32, (rows, PAGE_SIZE), 1)
    sn = jnp.where(j_pos <= t_pos, sn, -jnp.inf)
    acc = jnp.concatenate([acct_ref[...], jnp.zeros((D_C, LANE - rows), _F32)], axis=1).T[:rows]
    _, l2, acc2 = _softmax_step(sn, m_ref[...], l_ref[...], acc, kn[:, :D_C])
    o_ref[0] = _attn_out(acc2 / l2, steps, w_uv_ref, g_ref)


def _attn_sample(page_table, q, kv_new, cache_t, w_uv, g):
    nb, rows, _ = q.shape
    steps = kv_new.shape[1]
    n_pages = page_table.shape[1]
    const = lambda shape: pl.BlockSpec(shape, lambda b, pt: (0,) * len(shape), pipeline_mode=pl.Buffered(1))
    return pl.pallas_call(
        functools.partial(_attn_sample_kernel, steps=steps, n_pages=n_pages),
        grid_spec=pltpu.PrefetchScalarGridSpec(
            num_scalar_prefetch=1,
            grid=(nb,),
            in_specs=[pl.BlockSpec((1, rows, D_QK_PAD), lambda b, pt: (b, 0, 0)),
                      pl.BlockSpec((1, steps, D_QK_PAD), lambda b, pt: (b, 0, 0)),
                      const(w_uv.shape), const(g.shape), pl.BlockSpec(memory_space=pl.ANY)],
            out_specs=pl.BlockSpec((1, steps, D_ATT), lambda b, pt: (b, 0, 0)),
            scratch_shapes=[pltpu.VMEM((2, PAGES_PER_STEP, D_QK, PAGE_SIZE), _F32),
                            pltpu.SemaphoreType.DMA((2,)),
                            pltpu.VMEM((D_QK_PAD, PAGES_PER_STEP * PAGE_SIZE), _BF16),
                            pltpu.VMEM((PAGE_SIZE, D_QK_PAD), _BF16),
                            pltpu.VMEM((rows, 1), _F32), pltpu.VMEM((rows, 1), _F32),
                            pltpu.VMEM((D_C, rows), _F32)]),
        out_shape=jax.ShapeDtypeStruct((nb, steps, D_ATT), _BF16),
        compiler_params=pltpu.CompilerParams(dimension_semantics=("arbitrary",), vmem_limit_bytes=VMEM_LIMIT),
        name="attn_sample",
    )(page_table.reshape(-1), q, kv_new, w_uv, g, cache_t)


def _out_ffn_kernel(x_ref, attn_ref, conv_ref, prev_g_ref, prev_v_ref, w_oa_ref, w_oc_ref, g_ffn_ref,
                    w_up_g_ref, w_up_v_ref, w_dw_g_ref, w_dw_v_ref, b_dw_g_ref, b_dw_v_ref, w_down_ref, g_fin_ref,
                    y_ref, tail_g_ref, tail_v_ref, ubuf_ref, carry_ref, acc_ref, h2_ref, *, shift, tiles_per_seq):
    gb, sb, d_model = x_ref.shape
    tm = gb * sb
    hist = prev_g_ref.shape[1]
    j = pl.program_id(1)
    n_chunks = pl.num_programs(1)
    first = pl.program_id(0) % tiles_per_seq == 0

    @pl.when(j == 0)
    def _():
        flat = lambda ref: ref[...].reshape(tm, ref.shape[2])
        acc_ref[...] = flat(x_ref) + _dot(flat(attn_ref), w_oa_ref[...]) + _dot(flat(conv_ref), w_oc_ref[...])
        h2_ref[...] = _rms(acc_ref[...], g_ffn_ref[...]).astype(_BF16)

    halves = []
    for part, (prev_ref, w_up_ref, w_dw_ref, b_dw_ref, tail_ref) in enumerate((
            (prev_g_ref, w_up_g_ref, w_dw_g_ref, b_dw_g_ref, tail_g_ref),
            (prev_v_ref, w_up_v_ref, w_dw_v_ref, b_dw_v_ref, tail_v_ref))):
        slot = part * n_chunks + j

        @pl.when(first)
        def _():
            ubuf_ref[part, :hist] = prev_ref[0]

        @pl.when(jnp.logical_not(first))
        def _():
            ubuf_ref[part, :hist] = carry_ref[slot]

        ubuf_ref[part, hist:] = _dot(h2_ref[...], w_up_ref[...])
        last_rows = ubuf_ref[part, tm:]
        carry_ref[slot] = last_rows
        tail_ref[0] = last_rows
        halves.append(b_dw_ref[...]
                      + w_dw_ref[0:1] * ubuf_ref[part, hist - 2 * shift:hist - 2 * shift + tm]
                      + w_dw_ref[1:2] * ubuf_ref[part, hist - shift:hist - shift + tm]
                      + w_dw_ref[2:3] * ubuf_ref[part, hist:])
    act = (halves[0] * jax.nn.sigmoid(halves[0]) * halves[1]).astype(_BF16)
    acc_ref[...] += _dot(act, w_down_ref[...])

    @pl.when(j == n_chunks - 1)
    def _():
        y_ref[...] = _rms(acc_ref[...], g_fin_ref[...]).reshape(gb, sb, d_model)


def _out_ffn(x, attn_n, conv_n, prev, wts, *, block, shift, tiles_per_seq):
    g_dim, s_dim, d_model = x.shape
    gb, sb = block
    n_s = s_dim // sb
    n_seq, hist, _ = prev.shape
    w_oa, w_oc, g_ffn, w_up, w_dw, b_dw, w_down, g_fin = wts
    fc, nj = FFN_CHUNK, FFN_CHUNKS
    row_spec = lambda w: pl.BlockSpec((gb, sb, w), lambda i, j: (i // n_s, i % n_s, 0))
    const = lambda w: pl.BlockSpec(w.shape, lambda i, j: (0,) * w.ndim, pipeline_mode=pl.Buffered(1))
    hist_spec = lambda part: pl.BlockSpec((1, hist, fc), lambda i, j: (i // tiles_per_seq, 0, part * nj + j))
    cols_spec = lambda rows, part: pl.BlockSpec((rows, fc), lambda i, j: (0, part * nj + j))
    tail_spec = pl.BlockSpec((1, hist, fc), lambda i, j: (i, 0, j))
    n_tiles = (g_dim // gb) * n_s
    return pl.pallas_call(
        functools.partial(_out_ffn_kernel, shift=shift, tiles_per_seq=tiles_per_seq),
        grid=(n_tiles, nj),
        in_specs=[row_spec(d_model), row_spec(D_ATT), row_spec(D_CONV), hist_spec(0), hist_spec(1),
                  const(w_oa), const(w_oc), const(g_ffn),
                  cols_spec(d_model, 0), cols_spec(d_model, 1), cols_spec(SUBLANES, 0), cols_spec(SUBLANES, 1),
                  cols_spec(1, 0), cols_spec(1, 1), pl.BlockSpec((fc, d_model), lambda i, j: (j, 0)), const(g_fin)],
        out_specs=[row_spec(d_model), tail_spec, tail_spec],
        out_shape=[jax.ShapeDtypeStruct(x.shape, _F32)] + [jax.ShapeDtypeStruct((n_tiles, hist, D_FF), _F32)] * 2,
        scratch_shapes=[pltpu.VMEM((2, hist + gb * sb, fc), _F32), pltpu.VMEM((2 * nj, hist, fc), _F32),
                        pltpu.VMEM((gb * sb, d_model), _F32), pltpu.VMEM((gb * sb, d_model), _BF16)],
        compiler_params=pltpu.CompilerParams(dimension_semantics=("arbitrary", "arbitrary"),
                                             vmem_limit_bytes=VMEM_LIMIT),
        name="out_ffn",
    )(x, attn_n, conv_n, prev, prev, w_oa, w_oc, g_ffn, w_up, w_up, w_dw, w_dw, b_dw, b_dw, w_down, g_fin)


def _rope_tables(pos, lead=0):
    half = D_ROPE // 2
    inv = ROPE_THETA ** (-(jnp.arange(half, dtype=_F32) * 2.0 / D_ROPE))
    ang = pos.astype(_F32)[:, None] * inv[None, :]
    cos, sin = jnp.cos(ang), jnp.sin(ang)
    pad = ((0, 0), (0, LANE - D_ROPE - lead))
    one, zero = jnp.ones((pos.shape[0], lead), _F32), jnp.zeros((pos.shape[0], lead), _F32)
    return (jnp.pad(jnp.concatenate([one, cos, cos], axis=-1), pad),
            jnp.pad(jnp.concatenate([zero, -sin, sin], axis=-1), pad))


def _swap_halves(w):
    half = D_ROPE // 2
    return jnp.concatenate([w[..., half:], w[..., :half]], axis=-1)


def _pad_lanes(w, lead=0):
    return jnp.pad(w, [(0, 0)] * (w.ndim - 1) + [(lead, LANE - w.shape[-1] - lead)])


def kernel(x_prompt, x_sample, cache_kv_latent, state_conv, state_ffn_conv, page_table, g_attn_norm, w_in, g_q_norm, w_uq, g_kv_norm, w_uk, w_uv, w_dw, b_dw, g_conv_ln, b_conv_ln, g_attn_out, g_conv_out, w_o, g_ffn_norm, w_up, w_ffn_dw, b_ffn_dw, w_down, g_final):
    batch, seq, d_model = x_prompt.shape
    nb, steps, _ = x_sample.shape
    n_pages = page_table.shape[1]
    past_len = n_pages * PAGE_SIZE
    assert w_in.shape[0] == 1, "single-layer trunk"
    assert seq % Q_TILE == 0 and nb % CONV_ROWS == 0 and steps <= PAGE_SIZE
    assert n_pages % (2 * PAGES_PER_STEP) == 0, "the two-slot page buffer needs an even number of chunks per sequence"

    row = lambda v: v.reshape(1, -1)
    o1, o2, o3 = D_CQ, D_CQ + D_C, D_CQ + D_C + D_ROPE
    wi = w_in[0]
    w_in_ext = jnp.concatenate([wi[:, :o2], _pad_lanes(wi[:, o2:o3]), _pad_lanes(_swap_halves(wi[:, o2:o3])),
                                wi[:, o3:]], axis=1).astype(_BF16)
    wq = w_uq[0]
    w_qn = wq[:, :, :D_NOPE].reshape(D_CQ, N_HEADS * D_NOPE).astype(_BF16)
    w_qr = _pad_lanes(wq[:, :, D_NOPE:]).reshape(D_CQ, N_HEADS * LANE).astype(_BF16)
    w_qs = _pad_lanes(_swap_halves(wq[:, :, D_NOPE:])).reshape(D_CQ, N_HEADS * LANE).astype(_BF16)
    wk = jnp.transpose(w_uk[0], (1, 2, 0))
    sel = (jnp.arange(HEAD_GROUP)[None, :] == (jnp.arange(N_HEADS) % HEAD_GROUP)[:, None]).astype(_F32)
    w_uk_z = (sel[:, :, None, None] * wk[:, None]).reshape(N_HEADS, HEAD_GROUP * D_NOPE, D_C).astype(_BF16)
    wv = jnp.transpose(w_uv[0], (1, 0, 2))
    sel2 = (jnp.arange(2)[None, :] == (jnp.arange(N_HEADS) % 2)[:, None]).astype(_F32)
    w_uv_z = (wv[:, :, None, :] * sel2[:, None, :, None]).reshape(N_HEADS, D_C, 2 * D_V).astype(_BF16)
    in_wts = (row(g_attn_norm[0]), w_in_ext, row(g_q_norm[0]), w_qn, w_qr, w_qs, row(g_kv_norm[0]), w_uk_z)
    w_in_h = jnp.concatenate([wi[:, :o2], _pad_lanes(wi[:, o2:o3], D_NOPE),
                              _pad_lanes(_swap_halves(wi[:, o2:o3]), D_NOPE), wi[:, o3:]], axis=1).astype(_BF16)
    w_qa = _pad_lanes(wq).reshape(D_CQ, N_HEADS * HEAD_PAD).astype(_BF16)
    w_qb = _pad_lanes(_swap_halves(wq[:, :, D_NOPE:]), D_NOPE).reshape(D_CQ, N_HEADS * HEAD_PAD).astype(_BF16)
    w_kn = _pad_lanes(w_uk[0]).reshape(D_C, N_HEADS * HEAD_PAD).astype(_BF16)
    in_wts_h = (row(g_attn_norm[0]), w_in_h, row(g_q_norm[0]), w_qa, w_qb, row(g_kv_norm[0]), w_kn)
    conv_wts = (jnp.pad(w_dw[0], ((0, 1), (0, 0))), row(b_dw[0]), row(g_conv_ln[0]), row(b_conv_ln[0]),
                row(g_conv_out[0]))
    g_ao = row(g_attn_out[0])
    wo = w_o[0].astype(_BF16)
    w_fdw = jnp.pad(w_ffn_dw[0], ((0, SUBLANES - FFN_CONV_W), (0, 0)))
    ffn_wts = (wo[:D_ATT], wo[D_ATT:], row(g_ffn_norm[0]), w_up[0].astype(_BF16), w_fdw, row(b_ffn_dw[0]),
               w_down[0].astype(_BF16), row(g_final))

    rows_p = batch * seq
    tm = min(512, seq)
    xp = x_prompt.reshape(rows_p, d_model)
    cos_p, sin_p = _rope_tables(jnp.arange(seq, dtype=jnp.int32), lead=D_NOPE)
    q_p, kt_p, v_p, kv_p, glu_p = _inproj_heads(xp, cos_p, sin_p, in_wts_h, batch=batch, tm=tm, tq=Q_TILE)
    conv_p = _conv_prompt(glu_p, conv_wts, batch=batch, seq=seq, tc=tm)
    attn_p = _attn_prompt(q_p, kt_p, v_p, w_uv_z, g_ao, batch=batch, seq=seq, tk=min(512, seq))
    hist_p = SUBLANES
    prev_p = jnp.zeros((batch, hist_p, 2 * D_FF), _F32)
    tiled = lambda v: v.reshape(rows_p // tm, tm, v.shape[-1])
    y_p, tail_g, tail_v = _out_ffn(tiled(xp), tiled(attn_p), tiled(conv_p), prev_p, ffn_wts, block=(1, tm), shift=1,
                                   tiles_per_seq=seq // tm)
    last = seq // tm - 1
    tail_p = jnp.concatenate([tail_g[last::seq // tm], tail_v[last::seq // tm]], axis=-1)
    y_prompt = y_p.reshape(batch, seq, d_model)
    new_kv_prompt = kv_p.reshape(1, batch, seq, D_QK)
    new_conv_prompt = glu_p.reshape(batch, seq, D_CONV)[None, :, seq - (CONV_W - 1):]
    new_ffn_prompt = tail_p[None, :, hist_p - (FFN_CONV_W - 1):]

    rows_s = nb * steps
    xs = jnp.transpose(x_sample, (1, 0, 2)).reshape(rows_s, d_model)
    pos_s = past_len + jnp.arange(rows_s, dtype=jnp.int32) // nb
    cos_s, sin_s = _rope_tables(pos_s)
    tq_s = min(Q_TILE, nb)
    q_s, kv_s, kvb_s, glu_s = _inproj(xs, cos_s, sin_s, in_wts, tm=rows_s, tq=tq_s)
    q_s = q_s.reshape(steps, nb // tq_s, N_HEADS, tq_s, D_QK_PAD)
    q_s = jnp.transpose(q_s, (1, 3, 2, 0, 4)).reshape(nb, N_HEADS * steps, D_QK_PAD)
    kvn_s = jnp.transpose(kvb_s.reshape(steps, nb, D_QK_PAD), (1, 0, 2))
    ext = jnp.concatenate([jnp.transpose(state_conv[0], (1, 0, 2)), glu_s.reshape(steps, nb, D_CONV)], axis=0)
    conv_s = _conv_sample(ext, conv_wts, steps=steps).reshape(rows_s, D_CONV)
    attn_s = _attn_sample(page_table, q_s, kvn_s, jnp.swapaxes(cache_kv_latent[0], 1, 2), w_uv_z, g_ao)
    attn_s = jnp.transpose(attn_s, (1, 0, 2)).reshape(rows_s, D_ATT)
    sb = min(SAMPLE_FFN_SEQS, nb)
    n_tiles = nb // sb
    hist_s = (FFN_CONV_W - 1) * sb
    prev_s = jnp.transpose(state_ffn_conv[0].reshape(n_tiles, sb, FFN_CONV_W - 1, 2 * D_FF), (0, 2, 1, 3))
    prev_s = prev_s.reshape(n_tiles, hist_s, 2 * D_FF)
    steps3 = lambda v: v.reshape(steps, nb, v.shape[-1])
    y_s, tail_g, tail_v = _out_ffn(steps3(xs), steps3(attn_s), steps3(conv_s), prev_s, ffn_wts, block=(steps, sb),
                                   shift=sb, tiles_per_seq=1)
    tail_s = jnp.concatenate([tail_g, tail_v], axis=-1)
    y_sample = jnp.transpose(y_s, (1, 0, 2))
    new_kv_sample = jnp.transpose(kv_s.reshape(steps, nb, D_QK), (1, 0, 2))[None]
    new_conv_sample = jnp.transpose(ext[steps:], (1, 0, 2))[None]
    tail_s = tail_s.reshape(n_tiles, FFN_CONV_W - 1, sb, 2 * D_FF)
    new_ffn_sample = jnp.transpose(tail_s, (0, 2, 1, 3)).reshape(1, nb, FFN_CONV_W - 1, 2 * D_FF)
    return (y_prompt, y_sample, new_kv_prompt, new_conv_prompt, new_ffn_prompt,
            new_kv_sample, new_conv_sample, new_ffn_sample)
```
